```python
import jax, jax.numpy as jnp
from jax import lax
import numpy as np

D_MODEL = 1024
BATCH = 4
SEQ = 8192
DEPTH = 2

SSM_INNER = D_MODEL
SSM_HEAD_DIM = 64
SSM_HEADS = SSM_INNER // SSM_HEAD_DIM
SSM_GROUPS = 2
SSM_STATE = 128
SSM_CONV = 5
SSM_CHUNK = 128
SSM_CONV_DIM = SSM_INNER + 2 * SSM_GROUPS * SSM_STATE
RWKV_DIM = D_MODEL // 2
RWKV_HEAD_DIM = 64
RWKV_HEADS = RWKV_DIM // RWKV_HEAD_DIM
RWKV_W_RANK = 64
RWKV_A_RANK = 64
RWKV_G_RANK = 128
RWKV_COLS = 3 * RWKV_DIM + RWKV_W_RANK + RWKV_A_RANK + RWKV_G_RANK
RWKV_DECAY_SCALE = 0.6065306597
RWKV_LN_EPS = 64e-5
HGRN_DIM = D_MODEL // 2
HGRN_HEAD_DIM = 128
HGRN_HEADS = HGRN_DIM // HGRN_HEAD_DIM
HGRN_CHUNK = 64
HGRN_COLS = 5 * HGRN_DIM
N_BRANCH = 3
GATE_COLS = N_BRANCH * D_MODEL
N_IN = SSM_INNER + SSM_CONV_DIM + 2 * SSM_HEADS + RWKV_COLS + HGRN_COLS + GATE_COLS
D_FF = (8 * D_MODEL + 3 * 256 - 1) // (3 * 256) * 256
NORM_EPS = 1e-6

kernel_name = "hybrid_ssd_rwkv7_hgrn2_gated_encoder"


def _split_points(sizes):
    pts, acc = [], 0
    for s in sizes[:-1]:
        acc += s
        pts.append(acc)
    return pts


def rmsnorm(x, w, eps=NORM_EPS):
    xf = x.astype(jnp.float32)
    y = xf * lax.rsqrt(jnp.mean(xf * xf, axis=-1, keepdims=True) + eps)
    return (y * w.astype(jnp.float32)).astype(x.dtype)


def head_rmsnorm(x, w, n_heads, eps=NORM_EPS):
    b, l, d = x.shape
    xf = x.astype(jnp.float32).reshape(b, l, n_heads, d // n_heads)
    y = xf * lax.rsqrt(jnp.mean(xf * xf, axis=-1, keepdims=True) + eps)
    return (y.reshape(b, l, d) * w.astype(jnp.float32)).astype(x.dtype)


def flip_t(t):
    return jnp.flip(t, axis=1)


def centred_depthwise_conv(x, w, b):
    pad = (w.shape[0] - 1) // 2
    y = lax.conv_general_dilated(x, w, window_strides=(1,), padding=[(pad, pad)],
                                 dimension_numbers=('NWC', 'WIO', 'NWC'),
                                 feature_group_count=x.shape[-1])
    return y + b


def ssd_chunked(x, dt, a, bm, cm):
    b, L, H, P = x.shape
    G, N = bm.shape[2], bm.shape[3]
    hg = H // G
    c = L // SSM_CHUNK
    Q = SSM_CHUNK
    x = x.reshape(b, c, Q, G, hg, P)
    dt = dt.reshape(b, c, Q, G, hg)
    bm = bm.reshape(b, c, Q, G, N)
    cm = cm.reshape(b, c, Q, G, N)
    a_cum = jnp.cumsum(dt * a.reshape(G, hg), axis=2)
    xdt = x * dt[..., None]
    seg = a_cum[:, :, :, None] - a_cum[:, :, None]
    mask = jnp.tril(jnp.ones((Q, Q), dtype=bool))[:, :, None, None]
    lmat = jnp.exp(jnp.where(mask, seg, -jnp.inf))
    cb = jnp.einsum('bcqgn,bcsgn->bcqsg', cm, bm)
    y_diag = jnp.einsum('bcqsgh,bcsghp->bcqghp', cb[..., None] * lmat, xdt)
    decay_to_end = jnp.exp(a_cum[:, :, -1:] - a_cum)
    chunk_states = jnp.einsum('bcsgn,bcsghp->bcghpn', bm, xdt * decay_to_end[..., None])
    chunk_decay = jnp.exp(a_cum[:, :, -1])

    def step(s, inp):
        st, dec = inp
        return dec[..., None, None] * s + st, s

    s0 = jnp.zeros((b, G, hg, P, N), dtype=x.dtype)
    _, prev = lax.scan(step, s0, (jnp.moveaxis(chunk_states, 1, 0), jnp.moveaxis(chunk_decay, 1, 0)))
    prev = jnp.moveaxis(prev, 0, 1)
    y_off = jnp.einsum('bcqgn,bcghpn->bcqghp', cm, prev) * jnp.exp(a_cum)[..., None]
    return (y_diag + y_off).reshape(b, L, H, P)


def ssd_branch(z, xbc, dt_raw, conv_w, conv_b, dt_bias, a_log, d_skip, norm_w):
    b, L, _ = z.shape
    xbc = jax.nn.silu(centred_depthwise_conv(xbc, conv_w, conv_b))
    xs, bm, cm = jnp.split(xbc, [SSM_INNER, SSM_INNER + SSM_GROUPS * SSM_STATE], axis=-1)
    xs = xs.reshape(b, L, SSM_HEADS, SSM_HEAD_DIM)
    bm = bm.reshape(b, L, SSM_GROUPS, SSM_STATE)
    cm = cm.reshape(b, L, SSM_GROUPS, SSM_STATE)
    dt = jax.nn.softplus(dt_raw.reshape(b, L, 2, SSM_HEADS) + dt_bias)
    a = -jnp.exp(a_log)
    y_f = ssd_chunked(xs, dt[:, :, 0], a[0], bm, cm)
    y_b = flip_t(ssd_chunked(flip_t(xs), flip_t(dt[:, :, 1]), a[1], flip_t(bm), flip_t(cm)))
    y = y_f + y_b + xs * d_skip[:, None]
    y = y.reshape(b, L, SSM_INNER) * jax.nn.silu(z)
    return rmsnorm(y, norm_w)


def token_shift_bidir(p, mu):
    prev = jnp.pad(p[:, :-1], ((0, 0), (1, 0), (0, 0)))
    nxt = jnp.pad(p[:, 1:], ((0, 0), (0, 1), (0, 0)))
    return p + mu[0] * (prev - p) + mu[1] * (nxt - p)


def rwkv7_scan(r, w, k, v, kk, a, reverse):
    b = r.shape[0]
    s0 = jnp.zeros((b, RWKV_HEADS, RWKV_HEAD_DIM, RWKV_HEAD_DIM), dtype=r.dtype)

    def step(s, inp):
        r_t, w_t, k_t, v_t, kk_t, a_t = inp
        sa = jnp.einsum('bhvk,bhk->bhv', s, -kk_t)
        s = (s * w_t[:, :, None, :] + sa[..., None] * (kk_t * a_t)[:, :, None, :]
             + v_t[..., None] * k_t[:, :, None, :])
        return s, jnp.einsum('bhvk,bhk->bhv', s, r_t)

    xs = (jnp.moveaxis(r, 1, 0), jnp.moveaxis(w, 1, 0), jnp.moveaxis(k, 1, 0),
          jnp.moveaxis(v, 1, 0), jnp.moveaxis(kk, 1, 0), jnp.moveaxis(a, 1, 0))
    _, y = lax.scan(step, s0, xs, reverse=reverse)
    return jnp.moveaxis(y, 0, 1)


def rwkv7_branch(p, mu, w0, w_up, a0, a_up, g_up, k_k, k_a, r_k, ln_w, ln_b):
    b, L, _ = p.shape
    p = token_shift_bidir(p, mu)
    r, k, v, wd, ad, gd = jnp.split(p, _split_points(
        [RWKV_DIM, RWKV_DIM, RWKV_DIM, RWKV_W_RANK, RWKV_A_RANK, RWKV_G_RANK]), axis=-1)

    def heads(t):
        return t.reshape(b, L, RWKV_HEADS, RWKV_HEAD_DIM)

    tw = jnp.tanh(wd)
    w_f = jnp.exp(-RWKV_DECAY_SCALE * jax.nn.sigmoid(w0[0] + tw @ w_up[0]))
    w_b = jnp.exp(-RWKV_DECAY_SCALE * jax.nn.sigmoid(w0[1] + tw @ w_up[1]))
    a = jax.nn.sigmoid(a0 + ad @ a_up)
    g = jax.nn.sigmoid(gd) @ g_up
    kk = heads(k * k_k).astype(jnp.float32)
    kk = (kk * lax.rsqrt(jnp.sum(kk * kk, axis=-1, keepdims=True) + 1e-12)).astype(p.dtype)
    k = k * (1.0 + (a - 1.0) * k_a)
    r, k, v, a = heads(r), heads(k), heads(v), heads(a)
    y = (rwkv7_scan(r, heads(w_f), k, v, kk, a, False)
         + rwkv7_scan(r, heads(w_b), k, v, kk, a, True))
    yf = y.astype(jnp.float32)
    mean = jnp.mean(yf, axis=-1, keepdims=True)
    var = jnp.mean(jnp.square(yf - mean), axis=-1, keepdims=True)
    yn = ((yf - mean) * lax.rsqrt(var + RWKV_LN_EPS)).reshape(b, L, RWKV_DIM)
    yn = (yn * ln_w.astype(jnp.float32) + ln_b.astype(jnp.float32)).astype(p.dtype)
    r_k_h = r_k.reshape(RWKV_HEADS, RWKV_HEAD_DIM)
    bonus = jnp.sum(r * k * r_k_h, axis=-1, keepdims=True) * v
    return (yn + bonus.reshape(b, L, RWKV_DIM)) * g


def hgrn2_chunk_scan(q, k, v, log_f):
    b, L, h, dk = q.shape
    dv = v.shape[-1]
    c = L // HGRN_CHUNK

    def chunks(t):
        return jnp.transpose(t.reshape(b, c, HGRN_CHUNK, h, t.shape[-1]), (1, 0, 3, 2, 4))

    mask = jnp.tril(jnp.ones((HGRN_CHUNK, HGRN_CHUNK), dtype=bool))[:, :, None]

    def step(s, inp):
        q_c, k_c, v_c, g_c = inp
        cum = jnp.cumsum(g_c, axis=2)
        o_inter = jnp.einsum('bhtk,bhkv->bhtv', q_c * jnp.exp(cum), s)
        diff = cum[:, :, :, None, :] - cum[:, :, None, :, :]
        decay = jnp.exp(jnp.where(mask, diff, -jnp.inf))
        scores = jnp.einsum('bhtsk,bhsk->bhts', q_c[:, :, :, None, :] * decay, k_c)
        o_intra = jnp.einsum('bhts,bhsv->bhtv', scores, v_c)
        last = cum[:, :, -1:, :]
        s = (jnp.exp(last[:, :, 0, :])[..., None] * s
             + jnp.einsum('bhsk,bhsv->bhkv', k_c * jnp.exp(last - cum), v_c))
        return s, o_inter + o_intra

    s0 = jnp.zeros((b, h, dk, dv), dtype=q.dtype)
    _, o = lax.scan(step, s0, (chunks(q), chunks(k), chunks(v), chunks(log_f)))
    return jnp.transpose(o, (1, 0, 3, 2, 4)).reshape(b, L, h, dv)


def hgrn2_branch(p, lb, norm_w):
    b, L, _ = p.shape
    q, f_fwd, f_bwd, i, g = jnp.split(p, 5, axis=-1)

    def heads(t):
        return t.reshape(b, L, HGRN_HEADS, HGRN_HEAD_DIM)

    def forget(f):
        ff = lb + (1.0 - lb) * jax.nn.sigmoid(f)
        return heads(jnp.log(ff)), heads(1.0 - ff)

    logf_f, k_f = forget(f_fwd)
    logf_b, k_b = forget(f_bwd)
    qh, ih = heads(q), heads(i)
    o_f = hgrn2_chunk_scan(qh, k_f, ih, logf_f)
    o_b = flip_t(hgrn2_chunk_scan(flip_t(qh), flip_t(k_b), flip_t(ih), flip_t(logf_b)))
    o = (o_f + o_b).reshape(b, L, HGRN_DIM)
    return head_rmsnorm(o, norm_w, HGRN_HEADS) * jax.nn.silu(g)


def setup_inputs(seed: int = 0) -> dict:
    key = jax.random.key(seed)
    ks = iter(jax.random.split(key, 48))
    f32 = jnp.float32

    def nrm(shape, fan_in, scale=1.0):
        return scale * fan_in ** -0.5 * jax.random.normal(next(ks), shape, f32)

    def unif(shape, lo, hi):
        return jax.random.uniform(next(ks), shape, f32, lo, hi)

    def gain(shape):
        return 1.0 + 0.05 * jax.random.normal(next(ks), shape, f32)

    x = jax.random.normal(next(ks), (BATCH, SEQ, D_MODEL), f32)
    norm1_w = gain((DEPTH, D_MODEL))
    w_in = nrm((DEPTH, D_MODEL, N_IN), D_MODEL)
    ssm_conv_w = nrm((DEPTH, SSM_CONV, 1, SSM_CONV_DIM), SSM_CONV)
    ssm_conv_b = 0.02 * jax.random.normal(next(ks), (DEPTH, SSM_CONV_DIM), f32)
    dt0 = jnp.exp(unif((DEPTH, 2, SSM_HEADS), float(np.log(1e-3)), float(np.log(1e-1))))
    ssm_dt_bias = dt0 + jnp.log(-jnp.expm1(-dt0))
    ssm_a_log = jnp.log(unif((DEPTH, 2, SSM_HEADS), 1.0, 16.0))
    ssm_d = gain((DEPTH, SSM_HEADS))
    ssm_norm_w = gain((DEPTH, SSM_INNER))
    rwkv_mu = unif((DEPTH, 2, RWKV_COLS), 0.0, 0.5)
    rwkv_w0 = 0.5 * jax.random.normal(next(ks), (DEPTH, 2, RWKV_DIM), f32)
    rwkv_w_up = nrm((DEPTH, 2, RWKV_W_RANK, RWKV_DIM), RWKV_W_RANK, 0.5)
    rwkv_a0 = 0.1 * jax.random.normal(next(ks), (DEPTH, RWKV_DIM), f32)
    rwkv_a_up = nrm((DEPTH, RWKV_A_RANK, RWKV_DIM), RWKV_A_RANK, 0.1)
    rwkv_g_up = nrm((DEPTH, RWKV_G_RANK, RWKV_DIM), RWKV_G_RANK)
    rwkv_k_k = 0.85 + 0.05 * jax.random.normal(next(ks), (DEPTH, RWKV_DIM), f32)
    rwkv_k_a = gain((DEPTH, RWKV_DIM))
    rwkv_r_k = 0.1 * jax.random.normal(next(ks), (DEPTH, RWKV_DIM), f32)
    rwkv_ln_w = gain((DEPTH, RWKV_DIM))
    rwkv_ln_b = 0.02 * jax.random.normal(next(ks), (DEPTH, RWKV_DIM), f32)
    hgrn_lb_logits = 0.1 * jax.random.normal(next(ks), (DEPTH, HGRN_DIM), f32)
    hgrn_norm_w = gain((DEPTH, HGRN_DIM))
    w_branch_ssm = nrm((DEPTH, SSM_INNER, D_MODEL), SSM_INNER)
    w_branch_rwkv = nrm((DEPTH, RWKV_DIM, D_MODEL), RWKV_DIM)
    w_branch_hgrn = nrm((DEPTH, HGRN_DIM, D_MODEL), HGRN_DIM)
    w_out = nrm((DEPTH, D_MODEL, D_MODEL), D_MODEL)
    norm2_w = gain((DEPTH, D_MODEL))
    ffn_w_in = nrm((DEPTH, D_MODEL, 2 * D_FF), D_MODEL)
    ffn_w_down = nrm((DEPTH, D_FF, D_MODEL), D_FF)
    final_norm_w = gain((D_MODEL,))
    return {"x": x, "norm1_w": norm1_w, "w_in": w_in,
            "ssm_conv_w": ssm_conv_w, "ssm_conv_b": ssm_conv_b, "ssm_dt_bias": ssm_dt_bias,
            "ssm_a_log": ssm_a_log, "ssm_d": ssm_d, "ssm_norm_w": ssm_norm_w,
            "rwkv_mu": rwkv_mu, "rwkv_w0": rwkv_w0, "rwkv_w_up": rwkv_w_up, "rwkv_a0": rwkv_a0,
            "rwkv_a_up": rwkv_a_up, "rwkv_g_up": rwkv_g_up, "rwkv_k_k": rwkv_k_k,
            "rwkv_k_a": rwkv_k_a, "rwkv_r_k": rwkv_r_k, "rwkv_ln_w": rwkv_ln_w,
            "rwkv_ln_b": rwkv_ln_b, "hgrn_lb_logits": hgrn_lb_logits, "hgrn_norm_w": hgrn_norm_w,
            "w_branch_ssm": w_branch_ssm, "w_branch_rwkv": w_branch_rwkv,
            "w_branch_hgrn": w_branch_hgrn, "w_out": w_out, "norm2_w": norm2_w,
            "ffn_w_in": ffn_w_in, "ffn_w_down": ffn_w_down, "final_norm_w": final_norm_w}


def reference(x, norm1_w, w_in, ssm_conv_w, ssm_conv_b, ssm_dt_bias, ssm_a_log, ssm_d,
              ssm_norm_w, rwkv_mu, rwkv_w0, rwkv_w_up, rwkv_a0, rwkv_a_up, rwkv_g_up,
              rwkv_k_k, rwkv_k_a, rwkv_r_k, rwkv_ln_w, rwkv_ln_b, hgrn_lb_logits, hgrn_norm_w,
              w_branch_ssm, w_branch_rwkv, w_branch_hgrn, w_out, norm2_w, ffn_w_in,
              ffn_w_down, final_norm_w):
    b, L, _ = x.shape
    lb_p = jax.nn.softmax(hgrn_lb_logits.astype(jnp.float32), axis=0)
    lower_bounds = (jnp.cumsum(lb_p, axis=0) - lb_p[0]).astype(x.dtype)
    in_splits = _split_points([SSM_INNER, SSM_CONV_DIM, 2 * SSM_HEADS, RWKV_COLS, HGRN_COLS,
                               GATE_COLS])
    for l in range(DEPTH):
        xn = rmsnorm(x, norm1_w[l])
        p = xn @ w_in[l]
        z, xbc, dt_raw, p_rwkv, p_hgrn, p_gate = jnp.split(p, in_splits, axis=-1)
        y_ssm = ssd_branch(z, xbc, dt_raw, ssm_conv_w[l], ssm_conv_b[l], ssm_dt_bias[l],
                           ssm_a_log[l], ssm_d[l], ssm_norm_w[l]) @ w_branch_ssm[l]
        y_rwkv = rwkv7_branch(p_rwkv, rwkv_mu[l], rwkv_w0[l], rwkv_w_up[l], rwkv_a0[l],
                              rwkv_a_up[l], rwkv_g_up[l], rwkv_k_k[l], rwkv_k_a[l],
                              rwkv_r_k[l], rwkv_ln_w[l], rwkv_ln_b[l]) @ w_branch_rwkv[l]
        y_hgrn = hgrn2_branch(p_hgrn, lower_bounds[l], hgrn_norm_w[l]) @ w_branch_hgrn[l]
        gates = jax.nn.sigmoid(p_gate).reshape(b, L, N_BRANCH, D_MODEL)
        mixed = gates[:, :, 0] * y_ssm + gates[:, :, 1] * y_rwkv + gates[:, :, 2] * y_hgrn
        x = x + mixed @ w_out[l]
        h_gate, h_up = jnp.split(rmsnorm(x, norm2_w[l]) @ ffn_w_in[l], 2, axis=-1)
        x = x + (jax.nn.silu(h_gate) * h_up) @ ffn_w_down[l]
    return rmsnorm(x, final_norm_w)
```

```python
import functools

import numpy as np
import jax
import jax.numpy as jnp
from jax import lax
from jax.experimental import pallas as pl
from jax.experimental.pallas import tpu as pltpu

F32 = jnp.float32
BF16 = jnp.bfloat16

LANES = 128
SUBLANES = 8
VMEM_LIMIT_BYTES = 56 * 1024 * 1024

NORM_EPS = 1e-6
RWKV_DECAY_SCALE = 0.6065306597
RWKV_LN_EPS = 64e-5

HGRN_HEAD = 128
HGRN_CHUNK = 128
HGRN_LEVELS = (128, 64, 32, 16, 8, 4, 2)


def _dot(a, b, dims):
    return lax.dot_general(a, b, (dims, ((), ())), preferred_element_type=F32)


def _mm(a, b):
    return _dot(a.astype(BF16), b.astype(BF16), ((1,), (0,)))


def _mm_nt(a, b):
    return _dot(a.astype(BF16), b.astype(BF16), ((1,), (1,)))


def _mm_tn(a, b):
    return _dot(a.astype(BF16), b.astype(BF16), ((0,), (0,)))


def _split3(x):
    hi = x.astype(BF16)
    r1 = x - hi.astype(F32)
    mid = r1.astype(BF16)
    lo = (r1 - mid.astype(F32)).astype(BF16)
    return hi, mid, lo


def _sel_mm(sel, x):
    hi, mid, lo = _split3(x)
    dims = ((1,), (0,))
    return _dot(sel, hi, dims) + _dot(sel, mid, dims) + _dot(sel, lo, dims)


def _mm_sel(x, sel):
    hi, mid, lo = _split3(x)
    dims = ((1,), (0,))
    return _dot(hi, sel, dims) + _dot(mid, sel, dims) + _dot(lo, sel, dims)


def _sigmoid(x):
    return 1.0 / (1.0 + jnp.exp(-x))


def _silu(x):
    return x * _sigmoid(x)


def _hgrn_tables(reverse):
    c = HGRN_CHUNK
    nl = len(HGRN_LEVELS)
    sel = np.zeros(((nl + 2) * c + 16, c), np.float32)
    mask = np.zeros((nl + 1, c, c), np.float32)
    qrow = np.zeros((nl, c, 1), np.float32)
    t = np.arange(c)
    for li, n in enumerate(HGRN_LEVELS):
        mid = t // n * n + n // 2
        for i in range(c):
            m = mid[i]
            if not reverse:
                if i >= m:
                    sel[li * c + i, m:i + 1] = 1.0
                    qrow[li, i] = 1.0
                else:
                    sel[li * c + i, i + 1:m] = 1.0
            else:
                if i < m:
                    sel[li * c + i, i:m] = 1.0
                    qrow[li, i] = 1.0
                else:
                    sel[li * c + i, m:i] = 1.0
        same = (t[:, None] // n) == (t[None, :] // n)
        isq = qrow[li, :, 0] > 0.5
        mask[li] = same & isq[:, None] & (~isq)[None, :]
    mask[nl] = np.eye(c)
    for i in range(c):
        if not reverse:
            sel[nl * c + i, :i + 1] = 1.0
            sel[(nl + 1) * c + i, i + 1:] = 1.0
        else:
            sel[nl * c + i, i:] = 1.0
            sel[(nl + 1) * c + i, :i] = 1.0
    sel[(nl + 2) * c:, :] = 1.0
    return sel, mask, qrow


def _hgrn_chunk(q, k, v, logf, sel, mask_ref, qrow_ref, st_ref, d):
    c = HGRN_CHUNK
    nl = len(HGRN_LEVELS)
    e_all = _sel_mm(sel, logf)
    outs = []
    for h in range(q.shape[1] // HGRN_HEAD):
        cs = slice(h * HGRN_HEAD, (h + 1) * HGRN_HEAD)
        qh, kh, vh = q[:, cs], k[:, cs], v[:, cs]
        scores = _mm_nt(qh, kh) * mask_ref[nl]
        for li in range(nl):
            e = jnp.exp(e_all[li * c:(li + 1) * c, cs])
            x = (jnp.where(qrow_ref[li] > 0.5, qh, kh) * e).astype(BF16)
            scores = scores + _dot(x, x, ((1,), (1,))) * mask_ref[li]
        o = _mm(scores, vh)
        st = st_ref[d, h]
        qi = qh * jnp.exp(e_all[nl * c:(nl + 1) * c, cs])
        o = o + _mm_nt(qi, st)
        ks = kh * jnp.exp(e_all[(nl + 1) * c:(nl + 2) * c, cs])
        tot = jnp.exp(e_all[(nl + 2) * c:(nl + 2) * c + 1, cs])
        st_ref[d, h] = st * tot + _mm_tn(vh, ks)
        outs.append(o)
    return jnp.concatenate(outs, axis=1)


def _hgrn_kernel(pf_ref, pb_ref, lb_ref, self_ref, selb_ref, maskf_ref, maskb_ref, qrowf_ref, qrowb_ref,
                 of_ref, ob_ref, st_ref):
    @pl.when(pl.program_id(1) == 0)
    def _():
        st_ref[...] = jnp.zeros_like(st_ref)

    hd = of_ref.shape[-1]
    lb = lb_ref[...]
    nchunk = pf_ref.shape[0] // HGRN_CHUNK
    for d, (p_ref, o_ref, sel_ref, mask_ref, qrow_ref) in enumerate(
            ((pf_ref, of_ref, self_ref, maskf_ref, qrowf_ref), (pb_ref, ob_ref, selb_ref, maskb_ref, qrowb_ref))):
        order = range(nchunk) if d == 0 else range(nchunk - 1, -1, -1)
        for c in order:
            rows = pl.ds(c * HGRN_CHUNK, HGRN_CHUNK)
            q = p_ref[rows, 0:hd]
            fr = p_ref[rows, (1 + d) * hd:(2 + d) * hd]
            v = p_ref[rows, 3 * hd:4 * hd]
            ff = lb + (1.0 - lb) * _sigmoid(fr)
            o_ref[rows, :] = _hgrn_chunk(q, 1.0 - ff, v, jnp.log(ff), sel_ref[...], mask_ref, qrow_ref, st_ref, d)


def _hgrn_scan(p_hgrn, lb, block_t):
    b, L, w = p_hgrn.shape
    hd = w // 5
    nb = L // block_t
    tabs = [_hgrn_tables(rev) for rev in (False, True)]
    sel = [jnp.asarray(t[0], BF16) for t in tabs]
    mask = [jnp.asarray(t[1], F32) for t in tabs]
    qrow = [jnp.asarray(t[2], F32) for t in tabs]
    const2 = lambda bi, i: (0, 0)
    const3 = lambda bi, i: (0, 0, 0)
    out_sds = jax.ShapeDtypeStruct((b, L, hd), F32)
    return pl.pallas_call(
        _hgrn_kernel,
        grid=(b, nb),
        in_specs=[
            pl.BlockSpec((None, block_t, w), lambda bi, i: (bi, i, 0)),
            pl.BlockSpec((None, block_t, w), lambda bi, i: (bi, nb - 1 - i, 0)),
            pl.BlockSpec((1, hd), const2),
            pl.BlockSpec(sel[0].shape, const2),
            pl.BlockSpec(sel[1].shape, const2),
            pl.BlockSpec(mask[0].shape, const3),
            pl.BlockSpec(mask[1].shape, const3),
            pl.BlockSpec(qrow[0].shape, const3),
            pl.BlockSpec(qrow[1].shape, const3),
        ],
        out_specs=[
            pl.BlockSpec((None, block_t, hd), lambda bi, i: (bi, i, 0)),
            pl.BlockSpec((None, block_t, hd), lambda bi, i: (bi, nb - 1 - i, 0)),
        ],
        out_shape=[out_sds, out_sds],
        scratch_shapes=[pltpu.VMEM((2, hd // HGRN_HEAD, HGRN_HEAD, HGRN_HEAD), F32)],
        compiler_params=pltpu.CompilerParams(
            dimension_semantics=("arbitrary", "arbitrary"), vmem_limit_bytes=VMEM_LIMIT_BYTES),
        name="hgrn_scan",
    )(p_hgrn, p_hgrn, lb.reshape(1, hd), sel[0], sel[1], mask[0], mask[1], qrow[0], qrow[1])


SSD_CHUNK = 128
SSD_HEADS = 16
SSD_HEAD_DIM = 64
SSD_STATE = 128
SSD_GROUPS = 2
SSD_INNER = SSD_HEADS * SSD_HEAD_DIM
SSD_CONV_TAPS = 5
HALO = SUBLANES


def _halo_specs(block_t, width, nb, L, reverse):
    r = block_t // HALO
    last = L // HALO - 1
    blk = (lambda i: nb - 1 - i) if reverse else (lambda i: i)
    return [
        pl.BlockSpec((None, block_t, width), lambda bi, i: (bi, blk(i), 0)),
        pl.BlockSpec((None, HALO, width), lambda bi, i: (bi, jnp.maximum(blk(i) * r - 1, 0), 0)),
        pl.BlockSpec((None, HALO, width), lambda bi, i: (bi, jnp.minimum((blk(i) + 1) * r, last), 0)),
    ]


def _with_halo(main_ref, prev_ref, next_ref, is_first, is_last):
    prev = jnp.where(is_first, 0.0, prev_ref[...])
    nxt = jnp.where(is_last, 0.0, next_ref[...])
    return jnp.concatenate([prev, main_ref[...], nxt], axis=0)


def _ssd_direction(ext, dt_ref, cw_ref, cb_ref, dtb_ref, a_ref, tri_ref, exp_ref, dsk_ref, st_ref, o_ref, d):
    t_blk = dt_ref.shape[0]
    acc = cb_ref[...]
    for j in range(SSD_CONV_TAPS):
        off = HALO - (SSD_CONV_TAPS - 1) // 2 + j
        acc = acc + cw_ref[j:j + 1, :] * ext[off:off + t_blk, :]
    xact = _silu(acc)
    gn = SSD_GROUPS * SSD_STATE
    xs, bm, cm = xact[:, :SSD_INNER], xact[:, SSD_INNER:SSD_INNER + gn], xact[:, SSD_INNER + gn:]
    z = dt_ref[...] + dtb_ref[...]
    dt = jnp.maximum(z, 0.0) + jnp.log(1.0 + jnp.exp(-jnp.abs(z)))
    dta = dt * a_ref[...]
    expand = exp_ref[d]
    if d == 0:
        o_ref[...] = xs * dsk_ref[...]
    nchunk = t_blk // SSD_CHUNK
    hp = SSD_INNER // SSD_GROUPS
    row = lax.broadcasted_iota(jnp.int32, (SSD_CHUNK, SSD_CHUNK), 0)
    col = lax.broadcasted_iota(jnp.int32, (SSD_CHUNK, SSD_CHUNK), 1)
    causal = (row >= col) if d == 0 else (row <= col)
    lane = lax.broadcasted_iota(jnp.int32, (1, LANES), 1)
    lo = (lane < SSD_HEAD_DIM).astype(F32)
    hi = 1.0 - lo
    for c in (range(nchunk) if d == 0 else range(nchunk - 1, -1, -1)):
        rs = slice(c * SSD_CHUNK, (c + 1) * SSD_CHUNK)
        acum = _sel_mm(tri_ref[d], dta[rs])
        edge = acum[SSD_CHUNK - 1:SSD_CHUNK] if d == 0 else acum[0:1]
        acum_t = acum.T
        e_in = jnp.exp(acum)
        e_out = jnp.exp(edge - acum)
        wide = _mm_sel(jnp.concatenate([dt[rs], dt[rs] * e_out, e_in], axis=0), expand)
        xdt = xs[rs] * wide[:SSD_CHUNK]
        xout = xs[rs] * wide[SSD_CHUNK:2 * SSD_CHUNK]
        e_in_w = wide[2 * SSD_CHUNK:]
        dec = _mm_sel(jnp.broadcast_to(jnp.exp(edge), (SUBLANES, LANES)), expand)[0:1]
        y = []
        for g in range(SSD_GROUPS):
            bg = bm[rs, g * SSD_STATE:(g + 1) * SSD_STATE]
            cg = cm[rs, g * SSD_STATE:(g + 1) * SSD_STATE]
            cb = _mm_nt(cg, bg)
            gs = slice(g * hp, (g + 1) * hp)
            prev = st_ref[d, :, gs]
            y_off = _mm(cg, prev) * e_in_w[:, gs]
            st_ref[d, :, gs] = prev * dec[:, gs] + _mm_tn(bg, xout[:, gs])
            for pr in range(hp // LANES):
                ms = []
                for hh in range(2):
                    ln = 16 * d + g * (SSD_HEADS // SSD_GROUPS) + 2 * pr + hh
                    diff = acum[:, ln:ln + 1] - acum_t[ln:ln + 1, :]
                    ms.append((cb * jnp.exp(jnp.where(causal, diff, -jnp.inf))).astype(BF16))
                xp = xdt[:, g * hp + pr * LANES:g * hp + (pr + 1) * LANES]
                rhs = jnp.concatenate([xp * lo, xp * hi], axis=0).astype(BF16)
                y.append(_dot(jnp.concatenate(ms, axis=1), rhs, ((1,), (0,)))
                         + y_off[:, pr * LANES:(pr + 1) * LANES])
        yc = jnp.concatenate(y, axis=1)
        if d == 0:
            o_ref[rs, :] += yc
        else:
            o_ref[rs, :] = yc


def _ssd_kernel(xf_ref, xfp_ref, xfn_ref, xb_ref, xbp_ref, xbn_ref, dtf_ref, dtb_ref, cw_ref, cb_ref, bias_ref,
                a_ref, tri_ref, exp_ref, dsk_ref, of_ref, ob_ref, st_ref):
    i = pl.program_id(1)
    nb = pl.num_programs(1)

    @pl.when(i == 0)
    def _():
        st_ref[...] = jnp.zeros_like(st_ref)

    ext_f = _with_halo(xf_ref, xfp_ref, xfn_ref, i == 0, i == nb - 1)
    _ssd_direction(ext_f, dtf_ref, cw_ref, cb_ref, bias_ref, a_ref.at[0], tri_ref, exp_ref, dsk_ref, st_ref, of_ref, 0)
    ext_b = _with_halo(xb_ref, xbp_ref, xbn_ref, i == nb - 1, i == 0)
    _ssd_direction(ext_b, dtb_ref, cw_ref, cb_ref, bias_ref, a_ref.at[1], tri_ref, exp_ref, dsk_ref, st_ref, ob_ref, 1)


def _ssd_tables():
    c = SSD_CHUNK
    tri = np.stack([np.tril(np.ones((c, c), np.float32)), np.triu(np.ones((c, c), np.float32))])
    expand = np.zeros((2, LANES, SSD_INNER), np.float32)
    for d in range(2):
        for h in range(SSD_HEADS):
            expand[d, 16 * d + h, h * SSD_HEAD_DIM:(h + 1) * SSD_HEAD_DIM] = 1.0
    return tri, expand


def _ssd_scan(xbc, dt_pad, conv_w, conv_b, dt_bias, a_log, d_skip, block_t):
    b, L, w = xbc.shape
    nb = L // block_t
    tri, expand = _ssd_tables()
    taps = jnp.zeros((SUBLANES, w), F32).at[:SSD_CONV_TAPS].set(conv_w[:, 0, :])
    bias = jnp.zeros((1, LANES), F32).at[0, :2 * SSD_HEADS].set(dt_bias.reshape(-1))
    a = -jnp.exp(a_log)
    a_pad = jnp.zeros((2, 1, LANES), F32)
    a_pad = a_pad.at[0, 0, :SSD_HEADS].set(a[0]).at[1, 0, SSD_HEADS:2 * SSD_HEADS].set(a[1])
    dsk = jnp.repeat(d_skip, SSD_HEAD_DIM).reshape(1, SSD_INNER)
    const2 = lambda bi, i: (0, 0)
    const3 = lambda bi, i: (0, 0, 0)
    out_sds = jax.ShapeDtypeStruct((b, L, SSD_INNER), F32)
    return pl.pallas_call(
        _ssd_kernel,
        grid=(b, nb),
        in_specs=(
            _halo_specs(block_t, w, nb, L, False) + _halo_specs(block_t, w, nb, L, True) + [
                pl.BlockSpec((None, block_t, LANES), lambda bi, i: (bi, i, 0)),
                pl.BlockSpec((None, block_t, LANES), lambda bi, i: (bi, nb - 1 - i, 0)),
                pl.BlockSpec((SUBLANES, w), const2),
                pl.BlockSpec((1, w), const2),
                pl.BlockSpec((1, LANES), const2),
                pl.BlockSpec((2, 1, LANES), const3),
                pl.BlockSpec(tri.shape, const3),
                pl.BlockSpec(expand.shape, const3),
                pl.BlockSpec((1, SSD_INNER), const2),
            ]),
        out_specs=[
            pl.BlockSpec((None, block_t, SSD_INNER), lambda bi, i: (bi, i, 0)),
            pl.BlockSpec((None, block_t, SSD_INNER), lambda bi, i: (bi, nb - 1 - i, 0)),
        ],
        out_shape=[out_sds, out_sds],
        scratch_shapes=[pltpu.VMEM((2, SSD_STATE, SSD_INNER), F32)],
        compiler_params=pltpu.CompilerParams(
            dimension_semantics=("arbitrary", "arbitrary"), vmem_limit_bytes=VMEM_LIMIT_BYTES),
        name="ssd_scan",
    )(xbc, xbc, xbc, xbc, xbc, xbc, dt_pad, dt_pad, taps, conv_b.reshape(1, w), bias, a_pad,
      jnp.asarray(tri, BF16), jnp.asarray(expand, BF16), dsk)


RWKV_HEAD = 64
RWKV_CHUNK = 64
RWKV_DIM = 512
RWKV_DOUBLINGS = 5


def _mm_acc(a, b):
    ah = a.astype(BF16)
    al = (a - ah.astype(F32)).astype(BF16)
    bh = b.astype(BF16)
    bl = (b - bh.astype(F32)).astype(BF16)
    dims = ((1,), (0,))
    return _dot(ah, bh, dims) + _dot(ah, bl, dims) + _dot(al, bh, dims)


def _bd(y, lo, hi):
    return jnp.concatenate([y * lo, y * hi], axis=0)


def _seg_sum(x, seg):
    return jnp.concatenate(
        [_mm_sel(x[:, j * LANES:(j + 1) * LANES], seg) for j in range(x.shape[1] // LANES)], axis=1)


def _rwkv_direction(ext, mu_ref, w0_ref, wup_ref, a0_ref, aup_ref, gup_ref, kk_ref, ka_ref, rk_ref, seg_ref,
                    tri_ref, st_ref, o_ref, g_ref, bonus_ref, d):
    t_blk = o_ref.shape[0]
    dm = RWKV_DIM
    p = ext[HALO:HALO + t_blk]
    prev = ext[HALO - 1:HALO - 1 + t_blk]
    nxt = ext[HALO + 1:HALO + 1 + t_blk]
    ps = p + mu_ref[0:1, :] * (prev - p) + mu_ref[1:2, :] * (nxt - p)
    r, k0, v = ps[:, 0:dm], ps[:, dm:2 * dm], ps[:, 2 * dm:3 * dm]
    wa = ps[:, 3 * dm:3 * dm + LANES]
    lw = -RWKV_DECAY_SCALE * _sigmoid(w0_ref[d] + _mm(jnp.tanh(wa), wup_ref[d]))
    a = _sigmoid(a0_ref[...] + _mm(wa, aup_ref[...]))
    seg = seg_ref[...]
    kk0 = k0 * kk_ref[...]
    kk = kk0 * lax.rsqrt(_seg_sum(kk0 * kk0, seg) + 1e-12)
    k = k0 * (1.0 + (a - 1.0) * ka_ref[...])
    if d == 0:
        g_ref[...] = _mm(_sigmoid(ps[:, 3 * dm + LANES:3 * dm + 2 * LANES]), gup_ref[...])
        bonus_ref[...] = _seg_sum(r * k * rk_ref[...], seg) * v

    gc = _sel_mm(tri_ref[d], lw)
    en = jnp.exp(-gc)
    rb = r * jnp.exp(gc)
    ab = -(kk * jnp.exp(gc - lw))
    kb = k * en
    bb = (kk * a) * en

    c = RWKV_CHUNK
    tt = lax.broadcasted_iota(jnp.int32, (c, LANES), 0)
    ss = lax.broadcasted_iota(jnp.int32, (c, LANES), 1) & (RWKV_HEAD - 1)
    strict = ((tt > ss) if d == 0 else (tt < ss)).astype(F32)
    eye = (tt == ss).astype(F32)
    incl = strict + eye
    r2 = lax.broadcasted_iota(jnp.int32, (LANES, LANES), 0)
    c2 = lax.broadcasted_iota(jnp.int32, (LANES, LANES), 1)
    same_head = ((r2 >= RWKV_HEAD) == (c2 >= RWKV_HEAD)).astype(F32)
    eye2 = (r2 == c2).astype(F32)
    lane = lax.broadcasted_iota(jnp.int32, (1, LANES), 1)
    lo = (lane < RWKV_HEAD).astype(F32)
    hi = 1.0 - lo

    nchunk = t_blk // c
    for ci in (range(nchunk) if d == 0 else range(nchunk - 1, -1, -1)):
        rows = slice(ci * c, (ci + 1) * c)
        edge = (ci + 1) * c - 1 if d == 0 else ci * c
        gam = jnp.exp(gc[edge:edge + 1, :])
        for j in range(dm // LANES):
            ls = slice(j * LANES, (j + 1) * LANES)
            am, rm, km, bm, vm = ab[rows, ls], rb[rows, ls], kb[rows, ls], bb[rows, ls], v[rows, ls]
            gram = _mm_nt(jnp.concatenate([am, rm], axis=0),
                          jnp.concatenate([km * lo, km * hi, bm * lo, bm * hi], axis=0))
            a_ak = gram[:c, :LANES] * strict
            a_ab = gram[:c, LANES:] * strict
            a_rk = gram[c:, :LANES] * incl
            a_rb = gram[c:, LANES:] * incl
            pw = a_ab
            tm = eye + pw
            for _ in range(RWKV_DOUBLINGS):
                pw = _mm_acc(pw, _bd(pw, lo, hi))
                tm = tm + _mm_acc(tm, _bd(pw, lo, hi))
            rhs0 = _mm(a_ak, _bd(vm, lo, hi))
            au = _mm_acc(tm, jnp.concatenate([_bd(am, lo, hi), _bd(rhs0, lo, hi)], axis=1))
            at, uv = au[:, :LANES], au[:, LANES:]
            qt = rm + _mm(a_rb, _bd(at, lo, hi))
            yloc = _mm(jnp.concatenate([a_rk, a_rb], axis=1),
                       jnp.concatenate([_bd(vm, lo, hi), _bd(uv, lo, hi)], axis=0))
            mt = eye2 + same_head * _mm_tn(bm, at)
            nt = same_head * _mm_tn(jnp.concatenate([vm, uv], axis=0), jnp.concatenate([km, bm], axis=0))
            ht = st_ref[d, j]
            o_ref[rows, ls] = yloc + _mm_nt(qt, ht)
            st_ref[d, j] = (_mm_nt(ht, mt) + nt) * gam[:, ls]


def _rwkv_kernel(pf_ref, pfp_ref, pfn_ref, pb_ref, pbp_ref, pbn_ref, mu_ref, w0_ref, wup_ref, a0_ref, aup_ref,
                 gup_ref, kk_ref, ka_ref, rk_ref, seg_ref, tri_ref, of_ref, ob_ref, g_ref, bonus_ref, st_ref):
    i = pl.program_id(1)
    nb = pl.num_programs(1)

    @pl.when(i == 0)
    def _():
        st_ref[...] = jnp.zeros_like(st_ref)

    params = (mu_ref, w0_ref, wup_ref, a0_ref, aup_ref, gup_ref, kk_ref, ka_ref, rk_ref, seg_ref, tri_ref, st_ref)
    ext_f = _with_halo(pf_ref, pfp_ref, pfn_ref, i == 0, i == nb - 1)
    _rwkv_direction(ext_f, *params, of_ref, g_ref, bonus_ref, 0)
    ext_b = _with_halo(pb_ref, pbp_ref, pbn_ref, i == nb - 1, i == 0)
    _rwkv_direction(ext_b, *params, ob_ref, g_ref, bonus_ref, 1)


def _rwkv_scan(p_rwkv, mu, w0, w_up, a0, a_up, g_up, k_k, k_a, r_k, block_t):
    b, L, w = p_rwkv.shape
    dm = RWKV_DIM
    nb = L // block_t
    c = RWKV_CHUNK
    t = np.arange(block_t)
    same = (t[:, None] // c) == (t[None, :] // c)
    tri = np.stack([same & (t[None, :] <= t[:, None]), same & (t[None, :] >= t[:, None])]).astype(np.float32)
    l = np.arange(LANES)
    seg = ((l[:, None] // RWKV_HEAD) == (l[None, :] // RWKV_HEAD)).astype(np.float32)
    rank = w_up.shape[1]
    wup_pad = jnp.zeros((2, LANES, dm), F32).at[:, :rank].set(w_up).astype(BF16)
    aup_pad = jnp.zeros((LANES, dm), F32).at[rank:rank + a_up.shape[0]].set(a_up).astype(BF16)
    row = lambda x: x.reshape(1, dm)
    const2 = lambda bi, i: (0, 0)
    const3 = lambda bi, i: (0, 0, 0)
    out_sds = jax.ShapeDtypeStruct((b, L, dm), F32)
    fwd_spec = pl.BlockSpec((None, block_t, dm), lambda bi, i: (bi, i, 0))
    bwd_spec = pl.BlockSpec((None, block_t, dm), lambda bi, i: (bi, nb - 1 - i, 0))
    return pl.pallas_call(
        _rwkv_kernel,
        grid=(b, nb),
        in_specs=(
            _halo_specs(block_t, w, nb, L, False) + _halo_specs(block_t, w, nb, L, True) + [
                pl.BlockSpec((2, w), const2),
                pl.BlockSpec((2, 1, dm), const3),
                pl.BlockSpec((2, LANES, dm), const3),
                pl.BlockSpec((1, dm), const2),
                pl.BlockSpec((LANES, dm), const2),
                pl.BlockSpec((g_up.shape[0], dm), const2),
                pl.BlockSpec((1, dm), const2),
                pl.BlockSpec((1, dm), const2),
                pl.BlockSpec((1, dm), const2),
                pl.BlockSpec((LANES, LANES), const2),
                pl.BlockSpec(tri.shape, const3),
            ]),
        out_specs=[fwd_spec, bwd_spec, fwd_spec, fwd_spec],
        out_shape=[out_sds] * 4,
        scratch_shapes=[pltpu.VMEM((2, dm // LANES, LANES, LANES), F32)],
        compiler_params=pltpu.CompilerParams(
            dimension_semantics=("arbitrary", "arbitrary"), vmem_limit_bytes=VMEM_LIMIT_BYTES),
        name="rwkv_scan",
    )(p_rwkv, p_rwkv, p_rwkv, p_rwkv, p_rwkv, p_rwkv, mu, w0.reshape(2, 1, dm), wup_pad, row(a0), aup_pad,
      g_up.astype(BF16), row(k_k), row(k_a), row(r_k), jnp.asarray(seg, BF16), jnp.asarray(tri, BF16))


def _rms(x, w):
    return x * lax.rsqrt(jnp.mean(x * x, axis=-1, keepdims=True) + NORM_EPS) * w


def _norm_proj_kernel(x_ref, nw_ref, *refs):
    n = len(refs) // 2
    xn = _rms(x_ref[...], nw_ref[...]).astype(BF16)
    for w_ref, o_ref in zip(refs[:n], refs[n:]):
        o_ref[...] = _dot(xn, w_ref[...], ((1,), (0,)))


def _norm_proj(x, norm_w, weights, block_t):
    t, dmodel = x.shape
    const2 = lambda i: (0, 0)
    return pl.pallas_call(
        _norm_proj_kernel,
        grid=(t // block_t,),
        in_specs=[pl.BlockSpec((block_t, dmodel), lambda i: (i, 0)), pl.BlockSpec((1, dmodel), const2)]
        + [pl.BlockSpec(w.shape, const2) for w in weights],
        out_specs=[pl.BlockSpec((block_t, w.shape[1]), lambda i: (i, 0)) for w in weights],
        out_shape=[jax.ShapeDtypeStruct((t, w.shape[1]), F32) for w in weights],
        compiler_params=pltpu.CompilerParams(
            dimension_semantics=("arbitrary",), vmem_limit_bytes=VMEM_LIMIT_BYTES),
        name="norm_proj",
    )(x, norm_w.reshape(1, dmodel), *weights)


def _merge_kernel(x_ref, sf_ref, sb_ref, z_ref, rf_ref, rb_ref, rg_ref, rbon_ref, hf_ref, hb_ref, hg_ref, gate_ref,
                  snw_ref, lnw_ref, lnb_ref, hnw_ref, seg_ref, wbs_ref, wbr_ref, wbh_ref, wout_ref, o_ref):
    dmodel = x_ref.shape[1]
    ys = (sf_ref[...] + sb_ref[...]) * _silu(z_ref[...])
    y1 = _mm(_rms(ys, snw_ref[...]), wbs_ref[...])

    seg = seg_ref[...]
    yr = rf_ref[...] + rb_ref[...]
    cen = yr - _seg_sum(yr, seg) * (1.0 / RWKV_HEAD)
    var = _seg_sum(cen * cen, seg) * (1.0 / RWKV_HEAD)
    yn = cen * lax.rsqrt(var + RWKV_LN_EPS) * lnw_ref[...] + lnb_ref[...]
    y2 = _mm((yn + rbon_ref[...]) * rg_ref[...], wbr_ref[...])

    yh = hf_ref[...] + hb_ref[...]
    yh = jnp.concatenate(
        [_rms(yh[:, j * HGRN_HEAD:(j + 1) * HGRN_HEAD], hnw_ref[:, j * HGRN_HEAD:(j + 1) * HGRN_HEAD])
         for j in range(yh.shape[1] // HGRN_HEAD)], axis=1)
    y3 = _mm(yh * _silu(hg_ref[...]), wbh_ref[...])

    mixed = (_sigmoid(gate_ref[:, 0:dmodel]) * y1 + _sigmoid(gate_ref[:, dmodel:2 * dmodel]) * y2
             + _sigmoid(gate_ref[:, 2 * dmodel:3 * dmodel]) * y3)
    o_ref[...] = x_ref[...] + _mm(mixed, wout_ref[...])


def _merge(x, sf, sb, z, rf, rb, rg, rbon, hf, hb, p_hgrn, p_gate, ssm_norm_w, ln_w, ln_b, hgrn_norm_w,
           wb_ssm, wb_rwkv, wb_hgrn, w_out, block_t):
    t, dmodel = x.shape
    hd = hf.shape[1]
    l = np.arange(LANES)
    seg = jnp.asarray((l[:, None] // RWKV_HEAD) == (l[None, :] // RWKV_HEAD), BF16)
    tok = lambda w: pl.BlockSpec((block_t, w), lambda i: (i, 0))
    const2 = lambda i: (0, 0)
    full = lambda a: pl.BlockSpec(a.shape, const2)
    params = [ssm_norm_w.reshape(1, -1), ln_w.reshape(1, -1), ln_b.reshape(1, -1), hgrn_norm_w.reshape(1, -1), seg,
              wb_ssm, wb_rwkv, wb_hgrn, w_out]
    return pl.pallas_call(
        _merge_kernel,
        grid=(t // block_t,),
        in_specs=[tok(dmodel), tok(dmodel), tok(dmodel), tok(dmodel), tok(hd), tok(hd), tok(hd), tok(hd), tok(hd),
                  tok(hd), pl.BlockSpec((block_t, hd), lambda i: (i, 4)), tok(3 * dmodel)]
        + [full(a) for a in params],
        out_specs=tok(dmodel),
        out_shape=jax.ShapeDtypeStruct((t, dmodel), F32),
        compiler_params=pltpu.CompilerParams(
            dimension_semantics=("arbitrary",), vmem_limit_bytes=VMEM_LIMIT_BYTES),
        name="merge",
    )(x, sf, sb, z, rf, rb, rg, rbon, hf, hb, p_hgrn, p_gate, *params)


FFN_SPLIT = 2


def _ffn_kernel(x_ref, nw_ref, win_ref, wdown_ref, fw_ref, o_ref, *, final):
    x = x_ref[...]
    xn = _rms(x, nw_ref[...]).astype(BF16)
    dff = wdown_ref.shape[0]
    step = dff // FFN_SPLIT
    acc = x
    for c in range(FFN_SPLIT):
        hg = _dot(xn, win_ref[:, c * step:(c + 1) * step], ((1,), (0,)))
        hu = _dot(xn, win_ref[:, dff + c * step:dff + (c + 1) * step], ((1,), (0,)))
        acc = acc + _mm(_silu(hg) * hu, wdown_ref[c * step:(c + 1) * step, :])
    o_ref[...] = _rms(acc, fw_ref[...]) if final else acc


def _ffn(x, norm_w, w_in, w_down, final_w, final, block_t):
    t, dmodel = x.shape
    const2 = lambda i: (0, 0)
    tok = pl.BlockSpec((block_t, dmodel), lambda i: (i, 0))
    return pl.pallas_call(
        functools.partial(_ffn_kernel, final=final),
        grid=(t // block_t,),
        in_specs=[tok, pl.BlockSpec((1, dmodel), const2), pl.BlockSpec(w_in.shape, const2),
                  pl.BlockSpec(w_down.shape, const2), pl.BlockSpec((1, dmodel), const2)],
        out_specs=tok,
        out_shape=jax.ShapeDtypeStruct((t, dmodel), F32),
        compiler_params=pltpu.CompilerParams(
            dimension_semantics=("arbitrary",), vmem_limit_bytes=VMEM_LIMIT_BYTES),
        name="ffn",
    )(x, norm_w.reshape(1, dmodel), w_in, w_down, final_w.reshape(1, dmodel))


SCAN_BLOCK_T = 256
DENSE_BLOCK_T = 256


def kernel(x, norm1_w, w_in, ssm_conv_w, ssm_conv_b, ssm_dt_bias, ssm_a_log, ssm_d, ssm_norm_w, rwkv_mu, rwkv_w0, rwkv_w_up, rwkv_a0, rwkv_a_up, rwkv_g_up, rwkv_k_k, rwkv_k_a, rwkv_r_k, rwkv_ln_w, rwkv_ln_b, hgrn_lb_logits, hgrn_norm_w, w_branch_ssm, w_branch_rwkv, w_branch_hgrn, w_out, norm2_w, ffn_w_in, ffn_w_down, final_norm_w):
    b, L, dmodel = x.shape
    depth = w_in.shape[0]
    lb_p = jax.nn.softmax(hgrn_lb_logits.astype(F32), axis=0)
    lower_bounds = jnp.cumsum(lb_p, axis=0) - lb_p[0]
    conv_dim = ssm_conv_w.shape[-1]
    n_dt = ssm_dt_bias.shape[1] * ssm_dt_bias.shape[2]
    rw, hw = rwkv_mu.shape[-1], 5 * hgrn_norm_w.shape[-1]
    bounds = np.cumsum([0, SSD_INNER, conv_dim, n_dt, rw, hw, 3 * dmodel])
    seq = lambda a: a.reshape(b, L, a.shape[-1])
    xt = x.reshape(b * L, dmodel)
    for l in range(depth):
        w = w_in[l].astype(BF16)
        wz, wxbc, wdt, wrw, whg, wgate = (w[:, bounds[i]:bounds[i + 1]] for i in range(6))
        wdt = jnp.pad(wdt, ((0, 0), (0, LANES - n_dt)))
        z, xbc, dtp, p_rwkv = _norm_proj(xt, norm1_w[l], [wz, wxbc, wdt, wrw], DENSE_BLOCK_T)
        p_hgrn, p_gate = _norm_proj(xt, norm1_w[l], [whg, wgate], DENSE_BLOCK_T)
        sf, sb = _ssd_scan(seq(xbc), seq(dtp), ssm_conv_w[l], ssm_conv_b[l], ssm_dt_bias[l], ssm_a_log[l], ssm_d[l],
                           SCAN_BLOCK_T)
        rf, rb, rg, rbon = _rwkv_scan(seq(p_rwkv), rwkv_mu[l], rwkv_w0[l], rwkv_w_up[l], rwkv_a0[l], rwkv_a_up[l],
                                      rwkv_g_up[l], rwkv_k_k[l], rwkv_k_a[l], rwkv_r_k[l], SCAN_BLOCK_T)
        hf, hb = _hgrn_scan(seq(p_hgrn), lower_bounds[l], SCAN_BLOCK_T)
        flat = lambda a: a.reshape(b * L, a.shape[-1])
        xt = _merge(xt, flat(sf), flat(sb), z, flat(rf), flat(rb), flat(rg), flat(rbon), flat(hf), flat(hb),
                    p_hgrn, p_gate, ssm_norm_w[l], rwkv_ln_w[l], rwkv_ln_b[l], hgrn_norm_w[l],
                    w_branch_ssm[l].astype(BF16), w_branch_rwkv[l].astype(BF16), w_branch_hgrn[l].astype(BF16),
                    w_out[l].astype(BF16), DENSE_BLOCK_T)
        xt = _ffn(xt, norm2_w[l], ffn_w_in[l].astype(BF16), ffn_w_down[l].astype(BF16), final_norm_w,
                  l == depth - 1, DENSE_BLOCK_T)
    return xt.reshape(b, L, dmodel)
```

```python
import functools

import numpy as np
import jax
import jax.numpy as jnp
from jax import lax
from jax.experimental import pallas as pl
from jax.experimental.pallas import tpu as pltpu

F32 = jnp.float32
BF16 = jnp.bfloat16

LANES = 128
SUBLANES = 8
VMEM_LIMIT_BYTES = 56 * 1024 * 1024

NORM_EPS = 1e-6
RWKV_DECAY_SCALE = 0.6065306597
RWKV_LN_EPS = 64e-5

HGRN_HEAD = 128
HGRN_CHUNK = 128
HGRN_LEVELS = (128, 64, 32, 16, 8, 4, 2)


def _dot(a, b, dims):
    return lax.dot_general(a, b, (dims, ((), ())), preferred_element_type=F32)


def _mm(a, b):
    return _dot(a.astype(BF16), b.astype(BF16), ((1,), (0,)))


def _mm_nt(a, b):
    return _dot(a.astype(BF16), b.astype(BF16), ((1,), (1,)))


def _mm_tn(a, b):
    return _dot(a.astype(BF16), b.astype(BF16), ((0,), (0,)))


def _split3(x):
    hi = x.astype(BF16)
    r1 = x - hi.astype(F32)
    mid = r1.astype(BF16)
    lo = (r1 - mid.astype(F32)).astype(BF16)
    return hi, mid, lo


def _sel_mm(sel, x):
    hi, mid, lo = _split3(x)
    dims = ((1,), (0,))
    return _dot(sel, hi, dims) + _dot(sel, mid, dims) + _dot(sel, lo, dims)


def _sel_mm2(sel, x):
    hi = x.astype(BF16)
    lo = (x - hi.astype(F32)).astype(BF16)
    dims = ((1,), (0,))
    return _dot(sel, hi, dims) + _dot(sel, lo, dims)


def _mm_sel(x, sel):
    hi, mid, lo = _split3(x)
    dims = ((1,), (0,))
    return _dot(hi, sel, dims) + _dot(mid, sel, dims) + _dot(lo, sel, dims)


def _sigmoid(x):
    return 1.0 / (1.0 + jnp.exp(-x))


def _silu(x):
    return x * _sigmoid(x)


def _hgrn_tables(reverse):
    c = HGRN_CHUNK
    nl = len(HGRN_LEVELS)
    sel = np.zeros(((nl + 2) * c + 16, c), np.float32)
    mask = np.zeros((nl + 1, c, c), np.float32)
    qrow = np.zeros((nl, c, HGRN_HEAD), np.float32)
    t = np.arange(c)
    for li, n in enumerate(HGRN_LEVELS):
        mid = t // n * n + n // 2
        for i in range(c):
            m = mid[i]
            if not reverse:
                if i >= m:
                    sel[li * c + i, m:i + 1] = 1.0
                    qrow[li, i] = 1.0
                else:
                    sel[li * c + i, i + 1:m] = 1.0
            else:
                if i < m:
                    sel[li * c + i, i:m] = 1.0
                    qrow[li, i] = 1.0
                else:
                    sel[li * c + i, m:i] = 1.0
        same = (t[:, None] // n) == (t[None, :] // n)
        isq = qrow[li, :, 0] > 0.5
        mask[li] = same & isq[:, None] & (~isq)[None, :]
    mask[nl] = np.eye(c)
    for i in range(c):
        if not reverse:
            sel[nl * c + i, :i + 1] = 1.0
            sel[(nl + 1) * c + i, i + 1:] = 1.0
        else:
            sel[nl * c + i, i:] = 1.0
            sel[(nl + 1) * c + i, :i] = 1.0
    sel[(nl + 2) * c:, :] = 1.0
    return sel, mask, qrow


def _hgrn_chunk(q, k, v, logf, sel, mask_ref, qrow_ref, st_ref, d):
    c = HGRN_CHUNK
    nl = len(HGRN_LEVELS)
    e_all = _sel_mm2(sel, logf)
    outs = []
    for h in range(q.shape[1] // HGRN_HEAD):
        cs = slice(h * HGRN_HEAD, (h + 1) * HGRN_HEAD)
        qh, kh, vh = q[:, cs], k[:, cs], v[:, cs]
        scores = _mm_nt(qh, kh) * mask_ref[nl]
        for li in range(nl):
            e = jnp.exp(e_all[li * c:(li + 1) * c, cs])
            x = (jnp.where(qrow_ref[li] > 0.5, qh, kh) * e).astype(BF16)
            scores = scores + _dot(x, x, ((1,), (1,))) * mask_ref[li]
        o = _mm(scores, vh)
        st = st_ref[d, h]
        qi = qh * jnp.exp(e_all[nl * c:(nl + 1) * c, cs])
        o = o + _mm_nt(qi, st)
        ks = kh * jnp.exp(e_all[(nl + 1) * c:(nl + 2) * c, cs])
        tot = jnp.exp(e_all[(nl + 2) * c:(nl + 2) * c + 1, cs])
        st_ref[d, h] = st * tot + _mm_tn(vh, ks)
        outs.append(o)
    return jnp.concatenate(outs, axis=1)


def _hgrn_kernel(pf_ref, pb_ref, lb_ref, self_ref, selb_ref, maskf_ref, maskb_ref, qrowf_ref, qrowb_ref,
                 of_ref, ob_ref, st_ref):
    @pl.when(pl.program_id(1) == 0)
    def _():
        st_ref[...] = jnp.zeros_like(st_ref)

    hd = of_ref.shape[-1]
    lb = lb_ref[...]
    nchunk = pf_ref.shape[0] // HGRN_CHUNK
    for d, (p_ref, o_ref, sel_ref, mask_ref, qrow_ref) in enumerate(
            ((pf_ref, of_ref, self_ref, maskf_ref, qrowf_ref), (pb_ref, ob_ref, selb_ref, maskb_ref, qrowb_ref))):
        order = range(nchunk) if d == 0 else range(nchunk - 1, -1, -1)
        for c in order:
            rows = pl.ds(c * HGRN_CHUNK, HGRN_CHUNK)
            q = p_ref[rows, 0:hd]
            fr = p_ref[rows, (1 + d) * hd:(2 + d) * hd]
            v = p_ref[rows, 3 * hd:4 * hd]
            ff = lb + (1.0 - lb) * _sigmoid(fr)
            o_ref[rows, :] = _hgrn_chunk(q, 1.0 - ff, v, jnp.log(ff), sel_ref[...], mask_ref, qrow_ref, st_ref, d)


def _hgrn_scan(p_hgrn, lb, block_t):
    b, L, w = p_hgrn.shape
    hd = w // 4
    nb = L // block_t
    tabs = [_hgrn_tables(rev) for rev in (False, True)]
    sel = [jnp.asarray(t[0], BF16) for t in tabs]
    mask = [jnp.asarray(t[1], F32) for t in tabs]
    qrow = [jnp.asarray(t[2], F32) for t in tabs]
    const2 = lambda bi, i: (0, 0)
    const3 = lambda bi, i: (0, 0, 0)
    out_sds = jax.ShapeDtypeStruct((b, L, hd), F32)
    return pl.pallas_call(
        _hgrn_kernel,
        grid=(b, nb),
        in_specs=[
            pl.BlockSpec((None, block_t, w), lambda bi, i: (bi, i, 0)),
            pl.BlockSpec((None, block_t, w), lambda bi, i: (bi, nb - 1 - i, 0)),
            pl.BlockSpec((1, hd), const2),
            pl.BlockSpec(sel[0].shape, const2),
            pl.BlockSpec(sel[1].shape, const2),
            pl.BlockSpec(mask[0].shape, const3),
            pl.BlockSpec(mask[1].shape, const3),
            pl.BlockSpec(qrow[0].shape, const3),
            pl.BlockSpec(qrow[1].shape, const3),
        ],
        out_specs=[
            pl.BlockSpec((None, block_t, hd), lambda bi, i: (bi, i, 0)),
            pl.BlockSpec((None, block_t, hd), lambda bi, i: (bi, nb - 1 - i, 0)),
        ],
        out_shape=[out_sds, out_sds],
        scratch_shapes=[pltpu.VMEM((2, hd // HGRN_HEAD, HGRN_HEAD, HGRN_HEAD), F32)],
        compiler_params=pltpu.CompilerParams(
            dimension_semantics=("arbitrary", "arbitrary"), vmem_limit_bytes=VMEM_LIMIT_BYTES),
        name="hgrn_scan",
    )(p_hgrn, p_hgrn, lb.reshape(1, hd), sel[0], sel[1], mask[0], mask[1], qrow[0], qrow[1])


SSD_CHUNK = 128
SSD_HEADS = 16
SSD_HEAD_DIM = 64
SSD_STATE = 128
SSD_GROUPS = 2
SSD_INNER = SSD_HEADS * SSD_HEAD_DIM
SSD_CONV_TAPS = 5
HALO = SUBLANES


def _halo_specs(block_t, width, nb, L, reverse):
    r = block_t // HALO
    last = L // HALO - 1
    blk = (lambda i: nb - 1 - i) if reverse else (lambda i: i)
    return [
        pl.BlockSpec((None, block_t, width), lambda bi, i: (bi, blk(i), 0)),
        pl.BlockSpec((None, HALO, width), lambda bi, i: (bi, jnp.maximum(blk(i) * r - 1, 0), 0)),
        pl.BlockSpec((None, HALO, width), lambda bi, i: (bi, jnp.minimum((blk(i) + 1) * r, last), 0)),
    ]


def _with_halo(main_ref, prev_ref, next_ref, is_first, is_last):
    prev = jnp.where(is_first, 0.0, prev_ref[...])
    nxt = jnp.where(is_last, 0.0, next_ref[...])
    return jnp.concatenate([prev, main_ref[...], nxt], axis=0)


def _ssd_direction(ext, dt_ref, cw_ref, cb_ref, dtb_ref, a_ref, tri_ref, exp_ref, dsk_ref, st_ref, o_ref, d):
    t_blk = dt_ref.shape[0]
    acc = cb_ref[...]
    for j in range(SSD_CONV_TAPS):
        off = HALO - (SSD_CONV_TAPS - 1) // 2 + j
        acc = acc + cw_ref[j:j + 1, :] * ext[off:off + t_blk, :]
    xact = _silu(acc)
    gn = SSD_GROUPS * SSD_STATE
    xs, bm, cm = xact[:, :SSD_INNER], xact[:, SSD_INNER:SSD_INNER + gn], xact[:, SSD_INNER + gn:]
    z = dt_ref[...] + dtb_ref[...]
    dt = jnp.maximum(z, 0.0) + jnp.log(1.0 + jnp.exp(-jnp.abs(z)))
    dta = dt * a_ref[...]
    expand = exp_ref[d]
    if d == 0:
        o_ref[...] = xs * dsk_ref[...]
    nchunk = t_blk // SSD_CHUNK
    hp = SSD_INNER // SSD_GROUPS
    row = lax.broadcasted_iota(jnp.int32, (SSD_CHUNK, SSD_CHUNK), 0)
    col = lax.broadcasted_iota(jnp.int32, (SSD_CHUNK, SSD_CHUNK), 1)
    causal = (row >= col) if d == 0 else (row <= col)
    lane = lax.broadcasted_iota(jnp.int32, (1, LANES), 1)
    lo = (lane < SSD_HEAD_DIM).astype(F32)
    hi = 1.0 - lo
    for c in (range(nchunk) if d == 0 else range(nchunk - 1, -1, -1)):
        rs = slice(c * SSD_CHUNK, (c + 1) * SSD_CHUNK)
        acum = _sel_mm(tri_ref[d], dta[rs])
        edge = acum[SSD_CHUNK - 1:SSD_CHUNK] if d == 0 else acum[0:1]
        acum_t = acum.T
        e_in = jnp.exp(acum)
        e_out = jnp.exp(edge - acum)
        wide = _mm_sel(jnp.concatenate([dt[rs], dt[rs] * e_out, e_in], axis=0), expand)
        xdt = xs[rs] * wide[:SSD_CHUNK]
        xout = xs[rs] * wide[SSD_CHUNK:2 * SSD_CHUNK]
        e_in_w = wide[2 * SSD_CHUNK:]
        dec = _mm_sel(jnp.broadcast_to(jnp.exp(edge), (SUBLANES, LANES)), expand)[0:1]
        y = []
        for g in range(SSD_GROUPS):
            bg = bm[rs, g * SSD_STATE:(g + 1) * SSD_STATE]
            cg = cm[rs, g * SSD_STATE:(g + 1) * SSD_STATE]
            cb = _mm_nt(cg, bg)
            gs = slice(g * hp, (g + 1) * hp)
            prev = st_ref[d, :, gs]
            y_off = _mm(cg, prev) * e_in_w[:, gs]
            st_ref[d, :, gs] = prev * dec[:, gs] + _mm_tn(bg, xout[:, gs])
            for pr in range(hp // LANES):
                ms = []
                for hh in range(2):
                    ln = 16 * d + g * (SSD_HEADS // SSD_GROUPS) + 2 * pr + hh
                    diff = acum[:, ln:ln + 1] - acum_t[ln:ln + 1, :]
                    ms.append((cb * jnp.exp(jnp.where(causal, diff, -jnp.inf))).astype(BF16))
                xp = xdt[:, g * hp + pr * LANES:g * hp + (pr + 1) * LANES]
                rhs = jnp.concatenate([xp * lo, xp * hi], axis=0).astype(BF16)
                y.append(_dot(jnp.concatenate(ms, axis=1), rhs, ((1,), (0,)))
                         + y_off[:, pr * LANES:(pr + 1) * LANES])
        yc = jnp.concatenate(y, axis=1)
        if d == 0:
            o_ref[rs, :] += yc
        else:
            o_ref[rs, :] = yc


def _ssd_kernel(xf_ref, xfp_ref, xfn_ref, xb_ref, xbp_ref, xbn_ref, dtf_ref, dtb_ref, cw_ref, cb_ref, bias_ref,
                a_ref, tri_ref, exp_ref, dsk_ref, of_ref, ob_ref, st_ref):
    i = pl.program_id(1)
    nb = pl.num_programs(1)

    @pl.when(i == 0)
    def _():
        st_ref[...] = jnp.zeros_like(st_ref)

    ext_f = _with_halo(xf_ref, xfp_ref, xfn_ref, i == 0, i == nb - 1)
    _ssd_direction(ext_f, dtf_ref, cw_ref, cb_ref, bias_ref, a_ref.at[0], tri_ref, exp_ref, dsk_ref, st_ref, of_ref, 0)
    ext_b = _with_halo(xb_ref, xbp_ref, xbn_ref, i == nb - 1, i == 0)
    _ssd_direction(ext_b, dtb_ref, cw_ref, cb_ref, bias_ref, a_ref.at[1], tri_ref, exp_ref, dsk_ref, st_ref, ob_ref, 1)


def _ssd_tables():
    c = SSD_CHUNK
    tri = np.stack([np.tril(np.ones((c, c), np.float32)), np.triu(np.ones((c, c), np.float32))])
    expand = np.zeros((2, LANES, SSD_INNER), np.float32)
    for d in range(2):
        for h in range(SSD_HEADS):
            expand[d, 16 * d + h, h * SSD_HEAD_DIM:(h + 1) * SSD_HEAD_DIM] = 1.0
    return tri, expand


def _ssd_scan(xbc, dt_pad, conv_w, conv_b, dt_bias, a_log, d_skip, block_t):
    b, L, w = xbc.shape
    nb = L // block_t
    tri, expand = _ssd_tables()
    taps = jnp.zeros((SUBLANES, w), F32).at[:SSD_CONV_TAPS].set(conv_w[:, 0, :])
    bias = jnp.zeros((1, LANES), F32).at[0, :2 * SSD_HEADS].set(dt_bias.reshape(-1))
    a = -jnp.exp(a_log)
    a_pad = jnp.zeros((2, 1, LANES), F32)
    a_pad = a_pad.at[0, 0, :SSD_HEADS].set(a[0]).at[1, 0, SSD_HEADS:2 * SSD_HEADS].set(a[1])
    dsk = jnp.repeat(d_skip, SSD_HEAD_DIM).reshape(1, SSD_INNER)
    const2 = lambda bi, i: (0, 0)
    const3 = lambda bi, i: (0, 0, 0)
    out_sds = jax.ShapeDtypeStruct((b, L, SSD_INNER), F32)
    return pl.pallas_call(
        _ssd_kernel,
        grid=(b, nb),
        in_specs=(
            _halo_specs(block_t, w, nb, L, False) + _halo_specs(block_t, w, nb, L, True) + [
                pl.BlockSpec((None, block_t, LANES), lambda bi, i: (bi, i, 0)),
                pl.BlockSpec((None, block_t, LANES), lambda bi, i: (bi, nb - 1 - i, 0)),
                pl.BlockSpec((SUBLANES, w), const2),
                pl.BlockSpec((1, w), const2),
                pl.BlockSpec((1, LANES), const2),
                pl.BlockSpec((2, 1, LANES), const3),
                pl.BlockSpec(tri.shape, const3),
                pl.BlockSpec(expand.shape, const3),
                pl.BlockSpec((1, SSD_INNER), const2),
            ]),
        out_specs=[
            pl.BlockSpec((None, block_t, SSD_INNER), lambda bi, i: (bi, i, 0)),
            pl.BlockSpec((None, block_t, SSD_INNER), lambda bi, i: (bi, nb - 1 - i, 0)),
        ],
        out_shape=[out_sds, out_sds],
        scratch_shapes=[pltpu.VMEM((2, SSD_STATE, SSD_INNER), F32)],
        compiler_params=pltpu.CompilerParams(
            dimension_semantics=("arbitrary", "arbitrary"), vmem_limit_bytes=VMEM_LIMIT_BYTES),
        name="ssd_scan",
    )(xbc, xbc, xbc, xbc, xbc, xbc, dt_pad, dt_pad, taps, conv_b.reshape(1, w), bias, a_pad,
      jnp.asarray(tri, BF16), jnp.asarray(expand, BF16), dsk)


RWKV_HEAD = 64
RWKV_CHUNK = 64
RWKV_DIM = 512
RWKV_DOUBLINGS = 5


def _bd(y, lo):
    yb = y.astype(BF16)
    zero = jnp.zeros_like(yb)
    return jnp.concatenate([jnp.where(lo, yb, zero), jnp.where(lo, zero, yb)], axis=0)


def _seg_sum(x, seg):
    return jnp.concatenate(
        [_mm_sel(x[:, j * LANES:(j + 1) * LANES], seg) for j in range(x.shape[1] // LANES)], axis=1)


def _rwkv_local(ext, t_blk, mu_ref, w0_ref, wup_ref, a0_ref, aup_ref, gup_ref, kk_ref, ka_ref, rk_ref, seg_ref,
                tri_ref, g_ref, bonus_ref, d):
    dm = RWKV_DIM
    p = ext[HALO:HALO + t_blk]
    prev = ext[HALO - 1:HALO - 1 + t_blk]
    nxt = ext[HALO + 1:HALO + 1 + t_blk]
    ps = p + mu_ref[0:1, :] * (prev - p) + mu_ref[1:2, :] * (nxt - p)
    r, k0, v = ps[:, 0:dm], ps[:, dm:2 * dm], ps[:, 2 * dm:3 * dm]
    wa = ps[:, 3 * dm:3 * dm + LANES]
    lw = -RWKV_DECAY_SCALE * _sigmoid(w0_ref[d] + _mm(jnp.tanh(wa), wup_ref[d]))
    a = _sigmoid(a0_ref[...] + _mm(wa, aup_ref[...]))
    seg = seg_ref[...]
    kk0 = k0 * kk_ref[...]
    kk = kk0 * lax.rsqrt(_seg_sum(kk0 * kk0, seg) + 1e-12)
    k = k0 * (1.0 + (a - 1.0) * ka_ref[...])
    if d == 0:
        g_ref[...] = _mm(_sigmoid(ps[:, 3 * dm + LANES:3 * dm + 2 * LANES]), gup_ref[...])
        bonus_ref[...] = _seg_sum(r * k * rk_ref[...], seg) * v

    gc = _sel_mm2(tri_ref[d], lw)
    en = jnp.exp(-gc)
    rb = r * jnp.exp(gc)
    ab = -(kk * jnp.exp(gc - lw))
    kb = k * en
    bb = (kk * a) * en

    c = RWKV_CHUNK
    tt = lax.broadcasted_iota(jnp.int32, (c, LANES), 0)
    ss = lax.broadcasted_iota(jnp.int32, (c, LANES), 1) & (RWKV_HEAD - 1)
    strict = ((tt > ss) if d == 0 else (tt < ss)).astype(F32)
    eye = (tt == ss).astype(F32)
    incl = strict + eye
    r2 = lax.broadcasted_iota(jnp.int32, (LANES, LANES), 0)
    c2 = lax.broadcasted_iota(jnp.int32, (LANES, LANES), 1)
    same_head = ((r2 >= RWKV_HEAD) == (c2 >= RWKV_HEAD)).astype(F32)
    eye2 = (r2 == c2).astype(F32)
    lo = lax.broadcasted_iota(jnp.int32, (1, LANES), 1) < RWKV_HEAD
    lof = lo.astype(F32)
    hif = 1.0 - lof

    nchunk = t_blk // c
    items = [(ci, j) for ci in range(nchunk) for j in range(dm // LANES)]
    cut = lambda x, it: x[it[0] * c:(it[0] + 1) * c, it[1] * LANES:(it[1] + 1) * LANES]
    am = [cut(ab, it) for it in items]
    rm = [cut(rb, it) for it in items]
    km = [cut(kb, it) for it in items]
    bm = [cut(bb, it) for it in items]
    vm = [cut(v, it) for it in items]
    gram = [_mm_nt(jnp.concatenate([a_, r_], axis=0),
                   jnp.concatenate([k_ * lof, k_ * hif, b_ * lof, b_ * hif], axis=0))
            for a_, r_, k_, b_ in zip(am, rm, km, bm)]
    a_ak = [g[:c, :LANES] * strict for g in gram]
    a_rk = [g[c:, :LANES] * incl for g in gram]
    a_rb = [g[c:, LANES:] * incl for g in gram]
    pw = [g[:c, LANES:] * strict for g in gram]
    tm = [eye + p_ for p_ in pw]
    bdp = [_bd(p_, lo) for p_ in pw]
    for _ in range(RWKV_DOUBLINGS):
        pw = [_dot(p_.astype(BF16), b_, ((1,), (0,))) for p_, b_ in zip(pw, bdp)]
        bdp = [_bd(p_, lo) for p_ in pw]
        tm = [t_ + _dot(t_.astype(BF16), b_, ((1,), (0,))) for t_, b_ in zip(tm, bdp)]
    bdv = [_bd(v_, lo) for v_ in vm]
    rhs0 = [_dot(a_.astype(BF16), b_, ((1,), (0,))) for a_, b_ in zip(a_ak, bdv)]
    au = [_dot(t_.astype(BF16), jnp.concatenate([_bd(a_, lo), _bd(r_, lo)], axis=1), ((1,), (0,)))
          for t_, a_, r_ in zip(tm, am, rhs0)]
    at = [x[:, :LANES] for x in au]
    uv = [x[:, LANES:] for x in au]
    qt = [r_ + _dot(a_.astype(BF16), _bd(t_, lo), ((1,), (0,))) for r_, a_, t_ in zip(rm, a_rb, at)]
    yloc = [_dot(jnp.concatenate([k_, b_], axis=1).astype(BF16),
                 jnp.concatenate([bv_, _bd(u_, lo)], axis=0), ((1,), (0,)))
            for k_, b_, bv_, u_ in zip(a_rk, a_rb, bdv, uv)]
    mt = [eye2 + same_head * _mm_tn(b_, t_) for b_, t_ in zip(bm, at)]
    nt = [same_head * _mm_tn(jnp.concatenate([v_, u_], axis=0), jnp.concatenate([k_, b_], axis=0))
          for v_, u_, k_, b_ in zip(vm, uv, km, bm)]
    gam = []
    for ci in range(nchunk):
        edge = (ci + 1) * c - 1 if d == 0 else ci * c
        gam.append(jnp.exp(gc[edge:edge + 1, :]))
    return dict(yloc=yloc, qt=qt, mt=mt, nt=nt, gam=gam, items=items)


def _rwkv_kernel(pf_ref, pfp_ref, pfn_ref, pb_ref, pbp_ref, pbn_ref, mu_ref, w0_ref, wup_ref, a0_ref, aup_ref,
                 gup_ref, kk_ref, ka_ref, rk_ref, seg_ref, tri_ref, of_ref, ob_ref, g_ref, bonus_ref, st_ref):
    i = pl.program_id(1)
    nb = pl.num_programs(1)

    @pl.when(i == 0)
    def _():
        st_ref[...] = jnp.zeros_like(st_ref)

    params = (mu_ref, w0_ref, wup_ref, a0_ref, aup_ref, gup_ref, kk_ref, ka_ref, rk_ref, seg_ref, tri_ref)
    t_blk = of_ref.shape[0]
    ext_f = _with_halo(pf_ref, pfp_ref, pfn_ref, i == 0, i == nb - 1)
    ext_b = _with_halo(pb_ref, pbp_ref, pbn_ref, i == nb - 1, i == 0)
    loc = (_rwkv_local(ext_f, t_blk, *params, g_ref, bonus_ref, 0),
           _rwkv_local(ext_b, t_blk, *params, g_ref, bonus_ref, 1))

    c = RWKV_CHUNK
    nchunk = t_blk // c
    npair = RWKV_DIM // LANES
    o_refs = (of_ref, ob_ref)
    ht = [[st_ref[d, j] for j in range(npair)] for d in range(2)]
    for step in range(nchunk):
        for d in range(2):
            ci = step if d == 0 else nchunk - 1 - step
            for j in range(npair):
                n = ci * npair + j
                h = ht[d][j]
                o_refs[d][ci * c:(ci + 1) * c, j * LANES:(j + 1) * LANES] = (
                    loc[d]["yloc"][n] + _mm_nt(loc[d]["qt"][n], h))
                ht[d][j] = ((_mm_nt(h, loc[d]["mt"][n]) + loc[d]["nt"][n])
                            * loc[d]["gam"][ci][:, j * LANES:(j + 1) * LANES])
    for d in range(2):
        for j in range(npair):
            st_ref[d, j] = ht[d][j]


def _rwkv_scan(p_rwkv, mu, w0, w_up, a0, a_up, g_up, k_k, k_a, r_k, block_t):
    b, L, w = p_rwkv.shape
    dm = RWKV_DIM
    nb = L // block_t
    c = RWKV_CHUNK
    t = np.arange(block_t)
    same = (t[:, None] // c) == (t[None, :] // c)
    tri = np.stack([same & (t[None, :] <= t[:, None]), same & (t[None, :] >= t[:, None])]).astype(np.float32)
    l = np.arange(LANES)
    seg = ((l[:, None] // RWKV_HEAD) == (l[None, :] // RWKV_HEAD)).astype(np.float32)
    rank = w_up.shape[1]
    wup_pad = jnp.zeros((2, LANES, dm), F32).at[:, :rank].set(w_up).astype(BF16)
    aup_pad = jnp.zeros((LANES, dm), F32).at[rank:rank + a_up.shape[0]].set(a_up).astype(BF16)
    row = lambda x: x.reshape(1, dm)
    const2 = lambda bi, i: (0, 0)
    const3 = lambda bi, i: (0, 0, 0)
    out_sds = jax.ShapeDtypeStruct((b, L, dm), F32)
    fwd_spec = pl.BlockSpec((None, block_t, dm), lambda bi, i: (bi, i, 0))
    bwd_spec = pl.BlockSpec((None, block_t, dm), lambda bi, i: (bi, nb - 1 - i, 0))
    return pl.pallas_call(
        _rwkv_kernel,
        grid=(b, nb),
        in_specs=(
            _halo_specs(block_t, w, nb, L, False) + _halo_specs(block_t, w, nb, L, True) + [
                pl.BlockSpec((2, w), const2),
                pl.BlockSpec((2, 1, dm), const3),
                pl.BlockSpec((2, LANES, dm), const3),
                pl.BlockSpec((1, dm), const2),
                pl.BlockSpec((LANES, dm), const2),
                pl.BlockSpec((g_up.shape[0], dm), const2),
                pl.BlockSpec((1, dm), const2),
                pl.BlockSpec((1, dm), const2),
                pl.BlockSpec((1, dm), const2),
                pl.BlockSpec((LANES, LANES), const2),
                pl.BlockSpec(tri.shape, const3),
            ]),
        out_specs=[fwd_spec, bwd_spec, fwd_spec, fwd_spec],
        out_shape=[out_sds] * 4,
        scratch_shapes=[pltpu.VMEM((2, dm // LANES, LANES, LANES), F32)],
        compiler_params=pltpu.CompilerParams(
            dimension_semantics=("arbitrary", "arbitrary"), vmem_limit_bytes=VMEM_LIMIT_BYTES),
        name="rwkv_scan",
    )(p_rwkv, p_rwkv, p_rwkv, p_rwkv, p_rwkv, p_rwkv, mu, w0.reshape(2, 1, dm), wup_pad, row(a0), aup_pad,
      g_up.astype(BF16), row(k_k), row(k_a), row(r_k), jnp.asarray(seg, BF16), jnp.asarray(tri, BF16))


def _rms(x, w):
    return x * lax.rsqrt(jnp.mean(x * x, axis=-1, keepdims=True) + NORM_EPS) * w


def _norm_proj_kernel(x_ref, nw_ref, *refs):
    n = len(refs) // 2
    xn = _rms(x_ref[...], nw_ref[...]).astype(BF16)
    for w_ref, o_ref in zip(refs[:n], refs[n:]):
        o_ref[...] = _dot(xn, w_ref[...], ((1,), (0,)))


def _norm_proj(x, norm_w, weights, block_t):
    t, dmodel = x.shape
    const2 = lambda i: (0, 0)
    return pl.pallas_call(
        _norm_proj_kernel,
        grid=(t // block_t,),
        in_specs=[pl.BlockSpec((block_t, dmodel), lambda i: (i, 0)), pl.BlockSpec((1, dmodel), const2)]
        + [pl.BlockSpec(w.shape, const2) for w in weights],
        out_specs=[pl.BlockSpec((block_t, w.shape[1]), lambda i: (i, 0)) for w in weights],
        out_shape=[jax.ShapeDtypeStruct((t, w.shape[1]), F32) for w in weights],
        compiler_params=pltpu.CompilerParams(
            dimension_semantics=("arbitrary",), vmem_limit_bytes=VMEM_LIMIT_BYTES),
        name="norm_proj",
    )(x, norm_w.reshape(1, dmodel), *weights)


def _merge_kernel(x_ref, sf_ref, sb_ref, rf_ref, rb_ref, rg_ref, rbon_ref, hf_ref, hb_ref,
                  nw1_ref, wz_ref, whg_ref, wgate_ref, snw_ref, lnw_ref, lnb_ref, hnw_ref, seg_ref,
                  wbs_ref, wbr_ref, wbh_ref, wout_ref, o_ref):
    dmodel = x_ref.shape[1]
    dims = ((1,), (0,))
    xn = _rms(x_ref[...], nw1_ref[...]).astype(BF16)
    ys = (sf_ref[...] + sb_ref[...]) * _silu(_dot(xn, wz_ref[...], dims))
    y1 = _mm(_rms(ys, snw_ref[...]), wbs_ref[...])

    seg = seg_ref[...]
    yr = rf_ref[...] + rb_ref[...]
    cen = yr - _seg_sum(yr, seg) * (1.0 / RWKV_HEAD)
    var = _seg_sum(cen * cen, seg) * (1.0 / RWKV_HEAD)
    yn = cen * lax.rsqrt(var + RWKV_LN_EPS) * lnw_ref[...] + lnb_ref[...]
    y2 = _mm((yn + rbon_ref[...]) * rg_ref[...], wbr_ref[...])

    yh = hf_ref[...] + hb_ref[...]
    yh = jnp.concatenate(
        [_rms(yh[:, j * HGRN_HEAD:(j + 1) * HGRN_HEAD], hnw_ref[:, j * HGRN_HEAD:(j + 1) * HGRN_HEAD])
         for j in range(yh.shape[1] // HGRN_HEAD)], axis=1)
    y3 = _mm(yh * _silu(_dot(xn, whg_ref[...], dims)), wbh_ref[...])

    gates = _sigmoid(_dot(xn, wgate_ref[...], dims))
    mixed = (gates[:, 0:dmodel] * y1 + gates[:, dmodel:2 * dmodel] * y2 + gates[:, 2 * dmodel:3 * dmodel] * y3)
    o_ref[...] = x_ref[...] + _mm(mixed, wout_ref[...])


def _merge(x, sf, sb, rf, rb, rg, rbon, hf, hb, norm1_w, w_z, w_hg, w_gate, ssm_norm_w, ln_w, ln_b,
           hgrn_norm_w, wb_ssm, wb_rwkv, wb_hgrn, w_out, block_t):
    t, dmodel = x.shape
    hd = hf.shape[1]
    l = np.arange(LANES)
    seg = jnp.asarray((l[:, None] // RWKV_HEAD) == (l[None, :] // RWKV_HEAD), BF16)
    tok = lambda w: pl.BlockSpec((block_t, w), lambda i: (i, 0))
    const2 = lambda i: (0, 0)
    full = lambda a: pl.BlockSpec(a.shape, const2)
    params = [norm1_w.reshape(1, -1), w_z, w_hg, w_gate, ssm_norm_w.reshape(1, -1), ln_w.reshape(1, -1),
              ln_b.reshape(1, -1), hgrn_norm_w.reshape(1, -1), seg, wb_ssm, wb_rwkv, wb_hgrn, w_out]
    return pl.pallas_call(
        _merge_kernel,
        grid=(t // block_t,),
        in_specs=[tok(dmodel), tok(dmodel), tok(dmodel), tok(hd), tok(hd), tok(hd), tok(hd), tok(hd), tok(hd)]
        + [full(a) for a in params],
        out_specs=tok(dmodel),
        out_shape=jax.ShapeDtypeStruct((t, dmodel), F32),
        compiler_params=pltpu.CompilerParams(
            dimension_semantics=("arbitrary",), vmem_limit_bytes=VMEM_LIMIT_BYTES),
        name="merge",
    )(x, sf, sb, rf, rb, rg, rbon, hf, hb, *params)


FFN_SPLIT = 2


def _ffn_kernel(x_ref, nw_ref, win_ref, wdown_ref, fw_ref, o_ref, *, final):
    x = x_ref[...]
    xn = _rms(x, nw_ref[...]).astype(BF16)
    dff = wdown_ref.shape[0]
    step = dff // FFN_SPLIT
    acc = x
    for c in range(FFN_SPLIT):
        hg = _dot(xn, win_ref[:, c * step:(c + 1) * step], ((1,), (0,)))
        hu = _dot(xn, win_ref[:, dff + c * step:dff + (c + 1) * step], ((1,), (0,)))
        acc = acc + _mm(_silu(hg) * hu, wdown_ref[c * step:(c + 1) * step, :])
    o_ref[...] = _rms(acc, fw_ref[...]) if final else acc


def _ffn(x, norm_w, w_in, w_down, final_w, final, block_t):
    t, dmodel = x.shape
    const2 = lambda i: (0, 0)
    tok = pl.BlockSpec((block_t, dmodel), lambda i: (i, 0))
    return pl.pallas_call(
        functools.partial(_ffn_kernel, final=final),
        grid=(t // block_t,),
        in_specs=[tok, pl.BlockSpec((1, dmodel), const2), pl.BlockSpec(w_in.shape, const2),
                  pl.BlockSpec(w_down.shape, const2), pl.BlockSpec((1, dmodel), const2)],
        out_specs=tok,
        out_shape=jax.ShapeDtypeStruct((t, dmodel), F32),
        compiler_params=pltpu.CompilerParams(
            dimension_semantics=("arbitrary",), vmem_limit_bytes=VMEM_LIMIT_BYTES),
        name="ffn",
    )(x, norm_w.reshape(1, dmodel), w_in, w_down, final_w.reshape(1, dmodel))


SCAN_BLOCK_T = 256
DENSE_BLOCK_T = 256


def kernel(x, norm1_w, w_in, ssm_conv_w, ssm_conv_b, ssm_dt_bias, ssm_a_log, ssm_d, ssm_norm_w, rwkv_mu, rwkv_w0, rwkv_w_up, rwkv_a0, rwkv_a_up, rwkv_g_up, rwkv_k_k, rwkv_k_a, rwkv_r_k, rwkv_ln_w, rwkv_ln_b, hgrn_lb_logits, hgrn_norm_w, w_branch_ssm, w_branch_rwkv, w_branch_hgrn, w_out, norm2_w, ffn_w_in, ffn_w_down, final_norm_w):
    b, L, dmodel = x.shape
    depth = w_in.shape[0]
    lb_p = jax.nn.softmax(hgrn_lb_logits.astype(F32), axis=0)
    lower_bounds = jnp.cumsum(lb_p, axis=0) - lb_p[0]
    conv_dim = ssm_conv_w.shape[-1]
    n_dt = ssm_dt_bias.shape[1] * ssm_dt_bias.shape[2]
    rw, hw = rwkv_mu.shape[-1], 5 * hgrn_norm_w.shape[-1]
    bounds = np.cumsum([0, SSD_INNER, conv_dim, n_dt, rw, hw, 3 * dmodel])
    seq = lambda a: a.reshape(b, L, a.shape[-1])
    xt = x.reshape(b * L, dmodel)
    for l in range(depth):
        w = w_in[l].astype(BF16)
        wz, wxbc, wdt, wrw, whg, wgate = (w[:, bounds[i]:bounds[i + 1]] for i in range(6))
        wdt = jnp.pad(wdt, ((0, 0), (0, LANES - n_dt)))
        hd = hw // 5
        xbc, dtp, p_rwkv, p_hgrn = _norm_proj(xt, norm1_w[l], [wxbc, wdt, wrw, whg[:, :4 * hd]], DENSE_BLOCK_T)
        sf, sb = _ssd_scan(seq(xbc), seq(dtp), ssm_conv_w[l], ssm_conv_b[l], ssm_dt_bias[l], ssm_a_log[l], ssm_d[l],
                           SCAN_BLOCK_T)
        rf, rb, rg, rbon = _rwkv_scan(seq(p_rwkv), rwkv_mu[l], rwkv_w0[l], rwkv_w_up[l], rwkv_a0[l], rwkv_a_up[l],
                                      rwkv_g_up[l], rwkv_k_k[l], rwkv_k_a[l], rwkv_r_k[l], SCAN_BLOCK_T)
        hf, hb = _hgrn_scan(seq(p_hgrn), lower_bounds[l], SCAN_BLOCK_T)
        flat = lambda a: a.reshape(b * L, a.shape[-1])
        xt = _merge(xt, flat(sf), flat(sb), flat(rf), flat(rb), flat(rg), flat(rbon), flat(hf), flat(hb),
                    norm1_w[l], wz, whg[:, 4 * hd:], wgate, ssm_norm_w[l], rwkv_ln_w[l], rwkv_ln_b[l], hgrn_norm_w[l],
                    w_branch_ssm[l].astype(BF16), w_branch_rwkv[l].astype(BF16), w_branch_hgrn[l].astype(BF16),
                    w_out[l].astype(BF16), DENSE_BLOCK_T)
        xt = _ffn(xt, norm2_w[l], ffn_w_in[l].astype(BF16), ffn_w_down[l].astype(BF16), final_norm_w,
                  l == depth - 1, DENSE_BLOCK_T)
    return xt.reshape(b, L, dmodel)
```

```python
import functools

import numpy as np
import jax
import jax.numpy as jnp
from jax import lax
from jax.experimental import pallas as pl
from jax.experimental.pallas import tpu as pltpu

F32 = jnp.float32
BF16 = jnp.bfloat16

LANES = 128
SUBLANES = 8
VMEM_LIMIT_BYTES = 56 * 1024 * 1024

NORM_EPS = 1e-6
RWKV_DECAY_SCALE = 0.6065306597
RWKV_LN_EPS = 64e-5

HGRN_HEAD = 128
HGRN_CHUNK = 128
HGRN_LEVELS = (128, 64, 32, 16, 8, 4, 2)


def _dot(a, b, dims):
    return lax.dot_general(a, b, (dims, ((), ())), preferred_element_type=F32)


def _mm(a, b):
    return _dot(a.astype(BF16), b.astype(BF16), ((1,), (0,)))


def _mm_nt(a, b):
    return _dot(a.astype(BF16), b.astype(BF16), ((1,), (1,)))


def _mm_tn(a, b):
    return _dot(a.astype(BF16), b.astype(BF16), ((0,), (0,)))


def _split3(x):
    hi = x.astype(BF16)
    r1 = x - hi.astype(F32)
    mid = r1.astype(BF16)
    lo = (r1 - mid.astype(F32)).astype(BF16)
    return hi, mid, lo


def _sel_mm(sel, x):
    hi, mid, lo = _split3(x)
    dims = ((1,), (0,))
    return _dot(sel, hi, dims) + _dot(sel, mid, dims) + _dot(sel, lo, dims)


def _sel_mm2(sel, x):
    hi = x.astype(BF16)
    lo = (x - hi.astype(F32)).astype(BF16)
    dims = ((1,), (0,))
    return _dot(sel, hi, dims) + _dot(sel, lo, dims)


def _mm_sel(x, sel):
    hi, mid, lo = _split3(x)
    dims = ((1,), (0,))
    return _dot(hi, sel, dims) + _dot(mid, sel, dims) + _dot(lo, sel, dims)


def _const_spec(shape):
    zeros = (0,) * len(shape)
    return pl.BlockSpec(shape, lambda *_: zeros, pipeline_mode=pl.Buffered(1))


def _sigmoid(x):
    return 1.0 / (1.0 + jnp.exp(-x))


def _silu(x):
    return x * _sigmoid(x)


def _hgrn_tables(reverse):
    c = HGRN_CHUNK
    nl = len(HGRN_LEVELS)
    fine = [n for n in HGRN_LEVELS if n <= SUBLANES]
    cum = np.triu(np.ones((c, c), np.float32)) if reverse else np.tril(np.ones((c, c), np.float32))
    sel = np.zeros((len(fine) * c, c), np.float32)
    mask = np.zeros((nl + 1, c, c), np.float32)
    qrow = np.zeros((nl, c, HGRN_HEAD), np.float32)
    t = np.arange(c)
    for li, n in enumerate(HGRN_LEVELS):
        mid = t // n * n + n // 2
        qrow[li, (t < mid) if reverse else (t >= mid)] = 1.0
        if n in fine:
            fi = fine.index(n)
            for i in range(c):
                m = mid[i]
                if not reverse:
                    sel[fi * c + i, (m if i >= m else i + 1):(i + 1 if i >= m else m)] = 1.0
                else:
                    sel[fi * c + i, (i if i < m else m):(m if i < m else i)] = 1.0
        same = (t[:, None] // n) == (t[None, :] // n)
        isq = qrow[li, :, 0] > 0.5
        mask[li] = same & isq[:, None] & (~isq)[None, :]
    mask[nl] = np.eye(c)
    return cum, sel, mask, qrow


def _hgrn_chunk(q, k, v, logf, cum, sel, mask_ref, qrow_ref, st_ref, d):
    c = HGRN_CHUNK
    nl = len(HGRN_LEVELS)
    coarse = [n for n in HGRN_LEVELS if n > SUBLANES]
    cs = _sel_mm(cum, logf)
    e_fine = _sel_mm2(sel, logf)
    d_coarse = []
    for n in coarse:
        ref = jnp.concatenate(
            [jnp.broadcast_to(cs[r:r + 1, :], (n, cs.shape[1]))
             for r in range(n // 2 - (1 - d), c, n)], axis=0)
        d_coarse.append(cs - ref)
    edge = c - 1 if d == 0 else 0
    tot_log = cs[edge:edge + 1, :]
    outs = []
    for h in range(q.shape[1] // HGRN_HEAD):
        hs = slice(h * HGRN_HEAD, (h + 1) * HGRN_HEAD)
        qh, kh, vh = q[:, hs], k[:, hs], v[:, hs]
        scores = _mm_nt(qh, kh) * mask_ref[nl]
        for li in range(nl):
            isq = qrow_ref[li] > 0.5
            if li < len(coarse):
                dd = d_coarse[li][:, hs]
                e = jnp.exp(jnp.where(isq, dd, -dd))
            else:
                fi = li - len(coarse)
                e = jnp.exp(e_fine[fi * c:(fi + 1) * c, hs])
            x = (jnp.where(isq, qh, kh) * e).astype(BF16)
            scores = scores + _dot(x, x, ((1,), (1,))) * mask_ref[li]
        o = _mm(scores, vh)
        st = st_ref[d, h]
        o = o + _mm_nt(qh * jnp.exp(cs[:, hs]), st)
        ks = kh * jnp.exp(tot_log[:, hs] - cs[:, hs])
        st_ref[d, h] = st * jnp.exp(tot_log[:, hs]) + _mm_tn(vh, ks)
        outs.append(o)
    return jnp.concatenate(outs, axis=1)


def _hgrn_kernel(pf_ref, pb_ref, lb_ref, cum_ref, sel_ref, mask_ref, qrow_ref, of_ref, ob_ref, st_ref):
    @pl.when(pl.program_id(1) == 0)
    def _():
        st_ref[...] = jnp.zeros_like(st_ref)

    hd = of_ref.shape[-1]
    lb = lb_ref[...]
    nchunk = pf_ref.shape[0] // HGRN_CHUNK
    for d, (p_ref, o_ref) in enumerate(((pf_ref, of_ref), (pb_ref, ob_ref))):
        order = range(nchunk) if d == 0 else range(nchunk - 1, -1, -1)
        for c in order:
            rows = pl.ds(c * HGRN_CHUNK, HGRN_CHUNK)
            q = p_ref[rows, 0:hd]
            fr = p_ref[rows, (1 + d) * hd:(2 + d) * hd]
            v = p_ref[rows, 3 * hd:4 * hd]
            ff = lb + (1.0 - lb) * _sigmoid(fr)
            o_ref[rows, :] = _hgrn_chunk(q, 1.0 - ff, v, jnp.log(ff), cum_ref[d], sel_ref[d], mask_ref.at[d],
                                         qrow_ref.at[d], st_ref, d)


def _hgrn_scan(p_hgrn, lb, block_t):
    b, L, w = p_hgrn.shape
    hd = w // 4
    nb = L // block_t
    tabs = [_hgrn_tables(rev) for rev in (False, True)]
    cum, sel, mask, qrow = (jnp.asarray(np.stack([t[j] for t in tabs]), dt)
                            for j, dt in enumerate((BF16, BF16, F32, F32)))
    out_sds = jax.ShapeDtypeStruct((b, L, hd), F32)
    return pl.pallas_call(
        _hgrn_kernel,
        grid=(b, nb),
        in_specs=[
            pl.BlockSpec((None, block_t, w), lambda bi, i: (bi, i, 0)),
            pl.BlockSpec((None, block_t, w), lambda bi, i: (bi, nb - 1 - i, 0)),
            _const_spec((1, hd)), _const_spec(cum.shape), _const_spec(sel.shape), _const_spec(mask.shape),
            _const_spec(qrow.shape),
        ],
        out_specs=[
            pl.BlockSpec((None, block_t, hd), lambda bi, i: (bi, i, 0)),
            pl.BlockSpec((None, block_t, hd), lambda bi, i: (bi, nb - 1 - i, 0)),
        ],
        out_shape=[out_sds, out_sds],
        scratch_shapes=[pltpu.VMEM((2, hd // HGRN_HEAD, HGRN_HEAD, HGRN_HEAD), F32)],
        compiler_params=pltpu.CompilerParams(
            dimension_semantics=("arbitrary", "arbitrary"), vmem_limit_bytes=VMEM_LIMIT_BYTES),
        name="hgrn_scan",
    )(p_hgrn, p_hgrn, lb.reshape(1, hd), cum, sel, mask, qrow)


SSD_CHUNK = 128
SSD_HEADS = 16
SSD_HEAD_DIM = 64
SSD_STATE = 128
SSD_GROUPS = 2
SSD_INNER = SSD_HEADS * SSD_HEAD_DIM
SSD_CONV_TAPS = 5
HALO = SUBLANES


def _ssd_direction(x_ref, dt_ref, dtb_ref, a_ref, tri_ref, exp_ref, dsk_ref, st_ref, o_ref, d):
    t_blk = dt_ref.shape[0]
    gn = SSD_GROUPS * SSD_STATE
    z = dt_ref[...] + dtb_ref[...]
    dt = jnp.maximum(z, 0.0) + jnp.log(1.0 + jnp.exp(-jnp.abs(z)))
    dta = dt * a_ref[...]
    expand = exp_ref[d]
    nchunk = t_blk // SSD_CHUNK
    hp = SSD_INNER // SSD_GROUPS
    row = lax.broadcasted_iota(jnp.int32, (SSD_CHUNK, SSD_CHUNK), 0)
    col = lax.broadcasted_iota(jnp.int32, (SSD_CHUNK, SSD_CHUNK), 1)
    causal = (row >= col) if d == 0 else (row <= col)
    lane = lax.broadcasted_iota(jnp.int32, (1, LANES), 1)
    lo = (lane < SSD_HEAD_DIM).astype(F32)
    hi = 1.0 - lo
    for c in (range(nchunk) if d == 0 else range(nchunk - 1, -1, -1)):
        rs = slice(c * SSD_CHUNK, (c + 1) * SSD_CHUNK)
        acum = _sel_mm(tri_ref[d], dta[rs])
        edge = acum[SSD_CHUNK - 1:SSD_CHUNK] if d == 0 else acum[0:1]
        acum_t = acum.T
        e_in = jnp.exp(acum)
        e_out = jnp.exp(edge - acum)
        wide = _mm_sel(jnp.concatenate([dt[rs], dt[rs] * e_out, e_in], axis=0), expand)
        xs = x_ref[rs, :SSD_INNER]
        xdt = xs * wide[:SSD_CHUNK]
        xout = xs * wide[SSD_CHUNK:2 * SSD_CHUNK]
        e_in_w = wide[2 * SSD_CHUNK:]
        dec = _mm_sel(jnp.broadcast_to(jnp.exp(edge), (SUBLANES, LANES)), expand)[0:1]
        y = []
        for g in range(SSD_GROUPS):
            bg = x_ref[rs, SSD_INNER + g * SSD_STATE:SSD_INNER + (g + 1) * SSD_STATE]
            cg = x_ref[rs, SSD_INNER + gn + g * SSD_STATE:SSD_INNER + gn + (g + 1) * SSD_STATE]
            cb = _mm_nt(cg, bg)
            gs = slice(g * hp, (g + 1) * hp)
            prev = st_ref[d, :, gs]
            y_off = _mm(cg, prev) * e_in_w[:, gs]
            st_ref[d, :, gs] = prev * dec[:, gs] + _mm_tn(bg, xout[:, gs])
            for pr in range(hp // LANES):
                ms = []
                for hh in range(2):
                    ln = 16 * d + g * (SSD_HEADS // SSD_GROUPS) + 2 * pr + hh
                    diff = acum[:, ln:ln + 1] - acum_t[ln:ln + 1, :]
                    ms.append((cb * jnp.exp(jnp.where(causal, diff, -jnp.inf))).astype(BF16))
                xp = xdt[:, g * hp + pr * LANES:g * hp + (pr + 1) * LANES]
                rhs = jnp.concatenate([xp * lo, xp * hi], axis=0).astype(BF16)
                y.append(_dot(jnp.concatenate(ms, axis=1), rhs, ((1,), (0,)))
                         + y_off[:, pr * LANES:(pr + 1) * LANES])
        yc = jnp.concatenate(y, axis=1)
        o_ref[rs, :] = yc + xs * dsk_ref[...] if d == 0 else yc


def _ssd_kernel(xf_ref, xb_ref, dtf_ref, dtb_ref, bias_ref, a_ref, tri_ref, exp_ref, dsk_ref, of_ref, ob_ref,
                st_ref):
    @pl.when(pl.program_id(1) == 0)
    def _():
        st_ref[...] = jnp.zeros_like(st_ref)

    _ssd_direction(xf_ref, dtf_ref, bias_ref, a_ref.at[0], tri_ref, exp_ref, dsk_ref, st_ref, of_ref, 0)
    _ssd_direction(xb_ref, dtb_ref, bias_ref, a_ref.at[1], tri_ref, exp_ref, dsk_ref, st_ref, ob_ref, 1)


def _ssd_tables():
    c = SSD_CHUNK
    tri = np.stack([np.tril(np.ones((c, c), np.float32)), np.triu(np.ones((c, c), np.float32))])
    expand = np.zeros((2, LANES, SSD_INNER), np.float32)
    for d in range(2):
        for h in range(SSD_HEADS):
            expand[d, 16 * d + h, h * SSD_HEAD_DIM:(h + 1) * SSD_HEAD_DIM] = 1.0
    return tri, expand


def _ssd_scan(xact, dt_pad, dt_bias, a_log, d_skip, block_t):
    b, L, w = xact.shape
    nb = L // block_t
    tri, expand = _ssd_tables()
    bias = jnp.zeros((1, LANES), F32).at[0, :2 * SSD_HEADS].set(dt_bias.reshape(-1))
    a = -jnp.exp(a_log)
    a_pad = jnp.zeros((2, 1, LANES), F32)
    a_pad = a_pad.at[0, 0, :SSD_HEADS].set(a[0]).at[1, 0, SSD_HEADS:2 * SSD_HEADS].set(a[1])
    dsk = jnp.repeat(d_skip, SSD_HEAD_DIM).reshape(1, SSD_INNER)
    out_sds = jax.ShapeDtypeStruct((b, L, SSD_INNER), F32)
    fwd = lambda wd: pl.BlockSpec((None, block_t, wd), lambda bi, i: (bi, i, 0))
    bwd = lambda wd: pl.BlockSpec((None, block_t, wd), lambda bi, i: (bi, nb - 1 - i, 0))
    return pl.pallas_call(
        _ssd_kernel,
        grid=(b, nb),
        in_specs=[fwd(w), bwd(w), fwd(LANES), bwd(LANES), _const_spec((1, LANES)), _const_spec((2, 1, LANES)),
                  _const_spec(tri.shape), _const_spec(expand.shape), _const_spec((1, SSD_INNER))],
        out_specs=[
            pl.BlockSpec((None, block_t, SSD_INNER), lambda bi, i: (bi, i, 0)),
            pl.BlockSpec((None, block_t, SSD_INNER), lambda bi, i: (bi, nb - 1 - i, 0)),
        ],
        out_shape=[out_sds, out_sds],
        scratch_shapes=[pltpu.VMEM((2, SSD_STATE, SSD_INNER), F32)],
        compiler_params=pltpu.CompilerParams(
            dimension_semantics=("arbitrary", "arbitrary"), vmem_limit_bytes=VMEM_LIMIT_BYTES),
        name="ssd_scan",
    )(xact, xact, dt_pad, dt_pad, bias, a_pad, jnp.asarray(tri, BF16), jnp.asarray(expand, BF16), dsk)


RWKV_HEAD = 64
RWKV_CHUNK = 64
RWKV_DIM = 512
RWKV_DOUBLINGS = 5


def _bd(y, lo):
    yb = y.astype(BF16)
    zero = jnp.zeros_like(yb)
    return jnp.concatenate([jnp.where(lo, yb, zero), jnp.where(lo, zero, yb)], axis=0)


def _seg_sum(x, seg):
    return jnp.concatenate(
        [_mm_sel(x[:, j * LANES:(j + 1) * LANES], seg) for j in range(x.shape[1] // LANES)], axis=1)


def _rwkv_local(ps_ref, t_blk, w0_ref, wup_ref, a0_ref, aup_ref, gup_ref, kk_ref, ka_ref, rk_ref, seg_ref,
                tri_ref, g_ref, bonus_ref, d):
    dm = RWKV_DIM
    ps = ps_ref[...]
    r, k0, v = ps[:, 0:dm], ps[:, dm:2 * dm], ps[:, 2 * dm:3 * dm]
    wa = ps[:, 3 * dm:3 * dm + LANES]
    lw = -RWKV_DECAY_SCALE * _sigmoid(w0_ref[d] + _mm(jnp.tanh(wa), wup_ref[d]))
    a = _sigmoid(a0_ref[...] + _mm(wa, aup_ref[...]))
    seg = seg_ref[...]
    kk0 = k0 * kk_ref[...]
    kk = kk0 * lax.rsqrt(_seg_sum(kk0 * kk0, seg) + 1e-12)
    k = k0 * (1.0 + (a - 1.0) * ka_ref[...])
    if d == 0:
        g_ref[...] = _mm(_sigmoid(ps[:, 3 * dm + LANES:3 * dm + 2 * LANES]), gup_ref[...])
        bonus_ref[...] = _seg_sum(r * k * rk_ref[...], seg) * v

    gc = _sel_mm2(tri_ref[d], lw)
    en = jnp.exp(-gc)
    rb = r * jnp.exp(gc)
    ab = -(kk * jnp.exp(gc - lw))
    kb = k * en
    bb = (kk * a) * en

    c = RWKV_CHUNK
    tt = lax.broadcasted_iota(jnp.int32, (c, LANES), 0)
    ss = lax.broadcasted_iota(jnp.int32, (c, LANES), 1) & (RWKV_HEAD - 1)
    strict = ((tt > ss) if d == 0 else (tt < ss)).astype(F32)
    eye = (tt == ss).astype(F32)
    incl = strict + eye
    r2 = lax.broadcasted_iota(jnp.int32, (LANES, LANES), 0)
    c2 = lax.broadcasted_iota(jnp.int32, (LANES, LANES), 1)
    same_head = ((r2 >= RWKV_HEAD) == (c2 >= RWKV_HEAD)).astype(F32)
    eye2 = (r2 == c2).astype(F32)
    lo = lax.broadcasted_iota(jnp.int32, (1, LANES), 1) < RWKV_HEAD
    lof = lo.astype(F32)
    hif = 1.0 - lof

    nchunk = t_blk // c
    items = [(ci, j) for ci in range(nchunk) for j in range(dm // LANES)]
    cut = lambda x, it: x[it[0] * c:(it[0] + 1) * c, it[1] * LANES:(it[1] + 1) * LANES]
    am = [cut(ab, it) for it in items]
    rm = [cut(rb, it) for it in items]
    km = [cut(kb, it) for it in items]
    bm = [cut(bb, it) for it in items]
    vm = [cut(v, it) for it in items]
    gram = [_mm_nt(jnp.concatenate([a_, r_], axis=0),
                   jnp.concatenate([k_ * lof, k_ * hif, b_ * lof, b_ * hif], axis=0))
            for a_, r_, k_, b_ in zip(am, rm, km, bm)]
    a_ak = [g[:c, :LANES] * strict for g in gram]
    a_rk = [g[c:, :LANES] * incl for g in gram]
    a_rb = [g[c:, LANES:] * incl for g in gram]
    pw = [g[:c, LANES:] * strict for g in gram]
    tm = [eye + p_ for p_ in pw]
    pw = [_dot(p_.astype(BF16), _bd(p_, lo), ((1,), (0,))) for p_ in pw]
    for j in range(1, RWKV_DOUBLINGS + 1):
        if j < RWKV_DOUBLINGS:
            both = [_dot(p_.astype(BF16), jnp.concatenate([_bd(p_, lo), _bd(t_, lo)], axis=1), ((1,), (0,)))
                    for p_, t_ in zip(pw, tm)]
            pw = [x[:, :LANES] for x in both]
            tm = [t_ + x[:, LANES:] for t_, x in zip(tm, both)]
        else:
            tm = [t_ + _dot(p_.astype(BF16), _bd(t_, lo), ((1,), (0,))) for p_, t_ in zip(pw, tm)]
    bdv = [_bd(v_, lo) for v_ in vm]
    rhs0 = [_dot(a_.astype(BF16), b_, ((1,), (0,))) for a_, b_ in zip(a_ak, bdv)]
    au = [_dot(t_.astype(BF16), jnp.concatenate([_bd(a_, lo), _bd(r_, lo)], axis=1), ((1,), (0,)))
          for t_, a_, r_ in zip(tm, am, rhs0)]
    at = [x[:, :LANES] for x in au]
    uv = [x[:, LANES:] for x in au]
    qt = [r_ + _dot(a_.astype(BF16), _bd(t_, lo), ((1,), (0,))) for r_, a_, t_ in zip(rm, a_rb, at)]
    yloc = [_dot(jnp.concatenate([k_, b_], axis=1).astype(BF16),
                 jnp.concatenate([bv_, _bd(u_, lo)], axis=0), ((1,), (0,)))
            for k_, b_, bv_, u_ in zip(a_rk, a_rb, bdv, uv)]
    mt = [eye2 + same_head * _mm_tn(b_, t_) for b_, t_ in zip(bm, at)]
    nt = [same_head * _mm_tn(jnp.concatenate([v_, u_], axis=0), jnp.concatenate([k_, b_], axis=0))
          for v_, u_, k_, b_ in zip(vm, uv, km, bm)]
    gam = []
    for ci in range(nchunk):
        edge = (ci + 1) * c - 1 if d == 0 else ci * c
        gam.append(jnp.exp(gc[edge:edge + 1, :]))
    return dict(yloc=yloc, qt=qt, mt=mt, nt=nt, gam=gam, items=items)


def _rwkv_kernel(pf_ref, pb_ref, w0_ref, wup_ref, a0_ref, aup_ref, gup_ref, kk_ref, ka_ref, rk_ref, seg_ref,
                 tri_ref, of_ref, ob_ref, g_ref, bonus_ref, st_ref):
    @pl.when(pl.program_id(1) == 0)
    def _():
        st_ref[...] = jnp.zeros_like(st_ref)

    params = (w0_ref, wup_ref, a0_ref, aup_ref, gup_ref, kk_ref, ka_ref, rk_ref, seg_ref, tri_ref)
    t_blk = of_ref.shape[0]
    loc = (_rwkv_local(pf_ref, t_blk, *params, g_ref, bonus_ref, 0),
           _rwkv_local(pb_ref, t_blk, *params, g_ref, bonus_ref, 1))

    c = RWKV_CHUNK
    nchunk = t_blk // c
    npair = RWKV_DIM // LANES
    o_refs = (of_ref, ob_ref)
    ht = [[st_ref[d, j] for j in range(npair)] for d in range(2)]
    for step in range(nchunk):
        for d in range(2):
            ci = step if d == 0 else nchunk - 1 - step
            for j in range(npair):
                n = ci * npair + j
                h = ht[d][j]
                o_refs[d][ci * c:(ci + 1) * c, j * LANES:(j + 1) * LANES] = (
                    loc[d]["yloc"][n] + _mm_nt(loc[d]["qt"][n], h))
                ht[d][j] = ((_mm_nt(h, loc[d]["mt"][n]) + loc[d]["nt"][n])
                            * loc[d]["gam"][ci][:, j * LANES:(j + 1) * LANES])
    for d in range(2):
        for j in range(npair):
            st_ref[d, j] = ht[d][j]


def _rwkv_scan(ps, w0, w_up, a0, a_up, g_up, k_k, k_a, r_k, block_t):
    b, L, w = ps.shape
    dm = RWKV_DIM
    nb = L // block_t
    c = RWKV_CHUNK
    t = np.arange(block_t)
    same = (t[:, None] // c) == (t[None, :] // c)
    tri = np.stack([same & (t[None, :] <= t[:, None]), same & (t[None, :] >= t[:, None])]).astype(np.float32)
    l = np.arange(LANES)
    seg = ((l[:, None] // RWKV_HEAD) == (l[None, :] // RWKV_HEAD)).astype(np.float32)
    rank = w_up.shape[1]
    wup_pad = jnp.zeros((2, LANES, dm), F32).at[:, :rank].set(w_up).astype(BF16)
    aup_pad = jnp.zeros((LANES, dm), F32).at[rank:rank + a_up.shape[0]].set(a_up).astype(BF16)
    row = lambda x: x.reshape(1, dm)
    out_sds = jax.ShapeDtypeStruct((b, L, dm), F32)
    fwd = lambda wd: pl.BlockSpec((None, block_t, wd), lambda bi, i: (bi, i, 0))
    bwd = lambda wd: pl.BlockSpec((None, block_t, wd), lambda bi, i: (bi, nb - 1 - i, 0))
    return pl.pallas_call(
        _rwkv_kernel,
        grid=(b, nb),
        in_specs=[fwd(w), bwd(w), _const_spec((2, 1, dm)), _const_spec((2, LANES, dm)), _const_spec((1, dm)),
                  _const_spec((LANES, dm)), _const_spec((g_up.shape[0], dm)), _const_spec((1, dm)),
                  _const_spec((1, dm)), _const_spec((1, dm)), _const_spec((LANES, LANES)), _const_spec(tri.shape)],
        out_specs=[fwd(dm), bwd(dm), fwd(dm), fwd(dm)],
        out_shape=[out_sds] * 4,
        scratch_shapes=[pltpu.VMEM((2, dm // LANES, LANES, LANES), F32)],
        compiler_params=pltpu.CompilerParams(
            dimension_semantics=("arbitrary", "arbitrary"), vmem_limit_bytes=VMEM_LIMIT_BYTES),
        name="rwkv_scan",
    )(ps, ps, w0.reshape(2, 1, dm), wup_pad, row(a0), aup_pad, g_up.astype(BF16), row(k_k), row(k_a), row(r_k),
      jnp.asarray(seg, BF16), jnp.asarray(tri, BF16))


def _rms(x, w):
    return x * lax.rsqrt(jnp.mean(x * x, axis=-1, keepdims=True) + NORM_EPS) * w


def _in_proj_kernel(x_ref, xp_ref, xn_ref, nw_ref, wloc_ref, wpt_ref, cw_ref, cb_ref, mu_ref,
                    xact_ref, ps_ref, dt_ref, ph_ref, *, blocks_per_seq):
    i = pl.program_id(0)
    t = x_ref.shape[0]
    te = t + 2 * HALO
    dims = ((1,), (0,))
    xe = _rms(jnp.concatenate([xp_ref[...], x_ref[...], xn_ref[...]], axis=0), nw_ref[...])
    pt = _dot(xe[HALO:HALO + t].astype(BF16), wpt_ref[...], dims)
    dt_ref[...] = pt[:, :LANES]
    ph_ref[...] = pt[:, LANES:]

    pe = _dot(xe.astype(BF16), wloc_ref[...], dims)
    row = lax.broadcasted_iota(jnp.int32, (te, 1), 0)
    first = (i % blocks_per_seq) == 0
    last = (i % blocks_per_seq) == blocks_per_seq - 1
    outside = jnp.logical_or(jnp.logical_and(first, row < HALO), jnp.logical_and(last, row >= HALO + t))
    pe = jnp.where(outside, 0.0, pe)
    wc = cw_ref.shape[1]
    xb, pr = pe[:, :wc], pe[:, wc:]
    half = (SSD_CONV_TAPS - 1) // 2
    acc = cb_ref[...] + cw_ref[half:half + 1, :] * xb[HALO:HALO + t]
    for j in range(SSD_CONV_TAPS):
        if j != half:
            acc = acc + cw_ref[j:j + 1, :] * pltpu.roll(xb, (half - j) % te, 0)[HALO:HALO + t]
    xact_ref[...] = _silu(acc)
    p = pr[HALO:HALO + t]
    prev = pltpu.roll(pr, 1, 0)[HALO:HALO + t]
    nxt = pltpu.roll(pr, te - 1, 0)[HALO:HALO + t]
    ps_ref[...] = p + mu_ref[0:1, :] * (prev - p) + mu_ref[1:2, :] * (nxt - p)


def _in_proj(x, norm_w, w_conv, w_rwkv, w_dt, w_hgrn, conv_w, conv_b, mu, seq_len, block_t):
    t, dmodel = x.shape
    r = block_t // HALO
    last = t // HALO - 1
    wc, wr = w_conv.shape[1], w_rwkv.shape[1]
    w_loc = jnp.concatenate([w_conv, w_rwkv], axis=1)
    w_pt = jnp.concatenate([w_dt, w_hgrn], axis=1)
    taps = jnp.zeros((SUBLANES, wc), F32).at[:SSD_CONV_TAPS].set(conv_w[:, 0, :])
    tok = lambda wd: pl.BlockSpec((block_t, wd), lambda i: (i, 0))
    widths = (wc, wr, w_dt.shape[1], w_hgrn.shape[1])
    return pl.pallas_call(
        functools.partial(_in_proj_kernel, blocks_per_seq=seq_len // block_t),
        grid=(t // block_t,),
        in_specs=[tok(dmodel),
                  pl.BlockSpec((HALO, dmodel), lambda i: (jnp.maximum(i * r - 1, 0), 0)),
                  pl.BlockSpec((HALO, dmodel), lambda i: (jnp.minimum((i + 1) * r, last), 0)),
                  _const_spec((1, dmodel)), _const_spec(w_loc.shape), _const_spec(w_pt.shape),
                  _const_spec(taps.shape), _const_spec((1, wc)), _const_spec(mu.shape)],
        out_specs=[tok(wd) for wd in widths],
        out_shape=[jax.ShapeDtypeStruct((t, wd), F32) for wd in widths],
        compiler_params=pltpu.CompilerParams(
            dimension_semantics=("arbitrary",), vmem_limit_bytes=VMEM_LIMIT_BYTES),
        name="in_proj",
    )(x, x, x, norm_w.reshape(1, dmodel), w_loc, w_pt, taps, conv_b.reshape(1, wc), mu)


def _merge_kernel(x_ref, sf_ref, sb_ref, rf_ref, rb_ref, rg_ref, rbon_ref, hf_ref, hb_ref,
                  nw1_ref, wz_ref, whg_ref, wgate_ref, snw_ref, lnw_ref, lnb_ref, hnw_ref, seg_ref,
                  wbs_ref, wbr_ref, wbh_ref, wout_ref, o_ref):
    dmodel = x_ref.shape[1]
    dims = ((1,), (0,))
    xn = _rms(x_ref[...], nw1_ref[...]).astype(BF16)
    ys = (sf_ref[...] + sb_ref[...]) * _silu(_dot(xn, wz_ref[...], dims))
    y1 = _mm(_rms(ys, snw_ref[...]), wbs_ref[...])

    seg = seg_ref[...]
    yr = rf_ref[...] + rb_ref[...]
    cen = yr - _seg_sum(yr, seg) * (1.0 / RWKV_HEAD)
    var = _seg_sum(cen * cen, seg) * (1.0 / RWKV_HEAD)
    yn = cen * lax.rsqrt(var + RWKV_LN_EPS) * lnw_ref[...] + lnb_ref[...]
    y2 = _mm((yn + rbon_ref[...]) * rg_ref[...], wbr_ref[...])

    yh = hf_ref[...] + hb_ref[...]
    yh = jnp.concatenate(
        [_rms(yh[:, j * HGRN_HEAD:(j + 1) * HGRN_HEAD], hnw_ref[:, j * HGRN_HEAD:(j + 1) * HGRN_HEAD])
         for j in range(yh.shape[1] // HGRN_HEAD)], axis=1)
    y3 = _mm(yh * _silu(_dot(xn, whg_ref[...], dims)), wbh_ref[...])

    gates = _sigmoid(_dot(xn, wgate_ref[...], dims))
    mixed = (gates[:, 0:dmodel] * y1 + gates[:, dmodel:2 * dmodel] * y2 + gates[:, 2 * dmodel:3 * dmodel] * y3)
    o_ref[...] = x_ref[...] + _mm(mixed, wout_ref[...])


def _merge(x, sf, sb, rf, rb, rg, rbon, hf, hb, norm1_w, w_z, w_hg, w_gate, ssm_norm_w, ln_w, ln_b,
           hgrn_norm_w, wb_ssm, wb_rwkv, wb_hgrn, w_out, block_t):
    t, dmodel = x.shape
    hd = hf.shape[1]
    l = np.arange(LANES)
    seg = jnp.asarray((l[:, None] // RWKV_HEAD) == (l[None, :] // RWKV_HEAD), BF16)
    tok = lambda w: pl.BlockSpec((block_t, w), lambda i: (i, 0))
    full = lambda a: _const_spec(a.shape)
    params = [norm1_w.reshape(1, -1), w_z, w_hg, w_gate, ssm_norm_w.reshape(1, -1), ln_w.reshape(1, -1),
              ln_b.reshape(1, -1), hgrn_norm_w.reshape(1, -1), seg, wb_ssm, wb_rwkv, wb_hgrn, w_out]
    return pl.pallas_call(
        _merge_kernel,
        grid=(t // block_t,),
        in_specs=[tok(dmodel), tok(dmodel), tok(dmodel), tok(hd), tok(hd), tok(hd), tok(hd), tok(hd), tok(hd)]
        + [full(a) for a in params],
        out_specs=tok(dmodel),
        out_shape=jax.ShapeDtypeStruct((t, dmodel), F32),
        compiler_params=pltpu.CompilerParams(
            dimension_semantics=("arbitrary",), vmem_limit_bytes=VMEM_LIMIT_BYTES),
        name="merge",
    )(x, sf, sb, rf, rb, rg, rbon, hf, hb, *params)


FFN_SPLIT = 2


def _ffn_kernel(x_ref, nw_ref, win_ref, wdown_ref, fw_ref, o_ref, *, final):
    x = x_ref[...]
    xn = _rms(x, nw_ref[...]).astype(BF16)
    dff = wdown_ref.shape[0]
    step = dff // FFN_SPLIT
    acc = x
    for c in range(FFN_SPLIT):
        hg = _dot(xn, win_ref[:, c * step:(c + 1) * step], ((1,), (0,)))
        hu = _dot(xn, win_ref[:, dff + c * step:dff + (c + 1) * step], ((1,), (0,)))
        acc = acc + _mm(_silu(hg) * hu, wdown_ref[c * step:(c + 1) * step, :])
    o_ref[...] = _rms(acc, fw_ref[...]) if final else acc


def _ffn(x, norm_w, w_in, w_down, final_w, final, block_t):
    t, dmodel = x.shape
    tok = pl.BlockSpec((block_t, dmodel), lambda i: (i, 0))
    return pl.pallas_call(
        functools.partial(_ffn_kernel, final=final),
        grid=(t // block_t,),
        in_specs=[tok, _const_spec((1, dmodel)), _const_spec(w_in.shape), _const_spec(w_down.shape),
                  _const_spec((1, dmodel))],
        out_specs=tok,
        out_shape=jax.ShapeDtypeStruct((t, dmodel), F32),
        compiler_params=pltpu.CompilerParams(
            dimension_semantics=("arbitrary",), vmem_limit_bytes=VMEM_LIMIT_BYTES),
        name="ffn",
    )(x, norm_w.reshape(1, dmodel), w_in, w_down, final_w.reshape(1, dmodel))


SCAN_BLOCK_T = 256
DENSE_BLOCK_T = 256
FFN_BLOCK_T = 512


def kernel(x, norm1_w, w_in, ssm_conv_w, ssm_conv_b, ssm_dt_bias, ssm_a_log, ssm_d, ssm_norm_w, rwkv_mu, rwkv_w0, rwkv_w_up, rwkv_a0, rwkv_a_up, rwkv_g_up, rwkv_k_k, rwkv_k_a, rwkv_r_k, rwkv_ln_w, rwkv_ln_b, hgrn_lb_logits, hgrn_norm_w, w_branch_ssm, w_branch_rwkv, w_branch_hgrn, w_out, norm2_w, ffn_w_in, ffn_w_down, final_norm_w):
    b, L, dmodel = x.shape
    depth = w_in.shape[0]
    lb_p = jax.nn.softmax(hgrn_lb_logits.astype(F32), axis=0)
    lower_bounds = jnp.cumsum(lb_p, axis=0) - lb_p[0]
    conv_dim = ssm_conv_w.shape[-1]
    n_dt = ssm_dt_bias.shape[1] * ssm_dt_bias.shape[2]
    rw, hw = rwkv_mu.shape[-1], 5 * hgrn_norm_w.shape[-1]
    bounds = np.cumsum([0, SSD_INNER, conv_dim, n_dt, rw, hw, 3 * dmodel])
    seq = lambda a: a.reshape(b, L, a.shape[-1])
    xt = x.reshape(b * L, dmodel)
    for l in range(depth):
        w = w_in[l].astype(BF16)
        wz, wxbc, wdt, wrw, whg, wgate = (w[:, bounds[i]:bounds[i + 1]] for i in range(6))
        wdt = jnp.pad(wdt, ((0, 0), (0, LANES - n_dt)))
        hd = hw // 5
        xact, ps, dtp, p_hgrn = _in_proj(xt, norm1_w[l], wxbc, wrw, wdt, whg[:, :4 * hd], ssm_conv_w[l],
                                         ssm_conv_b[l], rwkv_mu[l], L, DENSE_BLOCK_T)
        sf, sb = _ssd_scan(seq(xact), seq(dtp), ssm_dt_bias[l], ssm_a_log[l], ssm_d[l], SCAN_BLOCK_T)
        rf, rb, rg, rbon = _rwkv_scan(seq(ps), rwkv_w0[l], rwkv_w_up[l], rwkv_a0[l], rwkv_a_up[l],
                                      rwkv_g_up[l], rwkv_k_k[l], rwkv_k_a[l], rwkv_r_k[l], SCAN_BLOCK_T)
        hf, hb = _hgrn_scan(seq(p_hgrn), lower_bounds[l], SCAN_BLOCK_T)
        flat = lambda a: a.reshape(b * L, a.shape[-1])
        xt = _merge(xt, flat(sf), flat(sb), flat(rf), flat(rb), flat(rg), flat(rbon), flat(hf), flat(hb),
                    norm1_w[l], wz, whg[:, 4 * hd:], wgate, ssm_norm_w[l], rwkv_ln_w[l], rwkv_ln_b[l], hgrn_norm_w[l],
                    w_branch_ssm[l].astype(BF16), w_branch_rwkv[l].astype(BF16), w_branch_hgrn[l].astype(BF16),
                    w_out[l].astype(BF16), DENSE_BLOCK_T)
        xt = _ffn(xt, norm2_w[l], ffn_w_in[l].astype(BF16), ffn_w_down[l].astype(BF16), final_norm_w,
                  l == depth - 1, FFN_BLOCK_T)
    return xt.reshape(b, L, dmodel)
```

```python
import functools

import numpy as np
import jax
import jax.numpy as jnp
from jax import lax
from jax.experimental import pallas as pl
from jax.experimental.pallas import tpu as pltpu

F32 = jnp.float32
BF16 = jnp.bfloat16

LANES = 128
SUBLANES = 8
VMEM_LIMIT_BYTES = 56 * 1024 * 1024

NORM_EPS = 1e-6
RWKV_DECAY_SCALE = 0.6065306597
RWKV_LN_EPS = 64e-5

HGRN_HEAD = 128
HGRN_CHUNK = 128
HGRN_LEVELS = (128, 64, 32, 16, 8, 4, 2)


def _dot(a, b, dims):
    return lax.dot_general(a, b, (dims, ((), ())), preferred_element_type=F32)


def _mm(a, b):
    return _dot(a.astype(BF16), b.astype(BF16), ((1,), (0,)))


def _mm_nt(a, b):
    return _dot(a.astype(BF16), b.astype(BF16), ((1,), (1,)))


def _mm_tn(a, b):
    return _dot(a.astype(BF16), b.astype(BF16), ((0,), (0,)))


def _split3(x):
    hi = x.astype(BF16)
    r1 = x - hi.astype(F32)
    mid = r1.astype(BF16)
    lo = (r1 - mid.astype(F32)).astype(BF16)
    return hi, mid, lo


def _sel_mm(sel3, x):
    return _dot(sel3, jnp.concatenate(_split3(x), axis=0), ((1,), (0,)))


def _sel_mm2(sel2, x):
    hi = x.astype(BF16)
    lo = (x - hi.astype(F32)).astype(BF16)
    return _dot(sel2, jnp.concatenate([hi, lo], axis=0), ((1,), (0,)))


def _sel_mm2_wide(sel, x):
    hi = x.astype(BF16)
    lo = (x - hi.astype(F32)).astype(BF16)
    dims = ((1,), (0,))
    return _dot(sel, hi, dims) + _dot(sel, lo, dims)


def _mm_sel(x, sel):
    hi, mid, lo = _split3(x)
    dims = ((1,), (0,))
    return _dot(hi, sel, dims) + _dot(mid, sel, dims) + _dot(lo, sel, dims)


def _const_spec(shape):
    zeros = (0,) * len(shape)
    return pl.BlockSpec(shape, lambda *_: zeros, pipeline_mode=pl.Buffered(1))


def _sigmoid(x):
    return 1.0 / (1.0 + jnp.exp(-x))


def _silu(x):
    return x * _sigmoid(x)


def _hgrn_tables(reverse):
    c = HGRN_CHUNK
    nl = len(HGRN_LEVELS)
    fine = [n for n in HGRN_LEVELS if n <= SUBLANES]
    cum = np.triu(np.ones((c, c), np.float32)) if reverse else np.tril(np.ones((c, c), np.float32))
    sel = np.zeros((len(fine) * c, c), np.float32)
    mask = np.zeros((nl + 1, c, c), np.float32)
    qrow = np.zeros((nl, c, HGRN_HEAD), np.float32)
    t = np.arange(c)
    for li, n in enumerate(HGRN_LEVELS):
        mid = t // n * n + n // 2
        qrow[li, (t < mid) if reverse else (t >= mid)] = 1.0
        if n in fine:
            fi = fine.index(n)
            for i in range(c):
                m = mid[i]
                if not reverse:
                    sel[fi * c + i, (m if i >= m else i + 1):(i + 1 if i >= m else m)] = 1.0
                else:
                    sel[fi * c + i, (i if i < m else m):(m if i < m else i)] = 1.0
        same = (t[:, None] // n) == (t[None, :] // n)
        isq = qrow[li, :, 0] > 0.5
        mask[li] = same & isq[:, None] & (~isq)[None, :]
    mask[nl] = np.eye(c)
    return np.tile(cum, (1, 3)), np.tile(sel, (1, 2)), mask, qrow


def _hgrn_chunk(q, k, v, logf, cum, sel, mask_ref, qrow_ref, st_ref, d):
    c = HGRN_CHUNK
    nl = len(HGRN_LEVELS)
    coarse = [n for n in HGRN_LEVELS if n > SUBLANES]
    cs = _sel_mm(cum, logf)
    e_fine = _sel_mm2(sel, logf)
    d_coarse = []
    for n in coarse:
        ref = jnp.concatenate(
            [jnp.broadcast_to(cs[r:r + 1, :], (n, cs.shape[1]))
             for r in range(n // 2 - (1 - d), c, n)], axis=0)
        d_coarse.append(cs - ref)
    edge = c - 1 if d == 0 else 0
    tot_log = cs[edge:edge + 1, :]
    outs = []
    for h in range(q.shape[1] // HGRN_HEAD):
        hs = slice(h * HGRN_HEAD, (h + 1) * HGRN_HEAD)
        qh, kh, vh = q[:, hs], k[:, hs], v[:, hs]
        scores = _mm_nt(qh, kh) * mask_ref[nl]
        for li in range(nl):
            isq = qrow_ref[li] > 0.5
            if li < len(coarse):
                dd = d_coarse[li][:, hs]
                e = jnp.exp(jnp.where(isq, dd, -dd))
            else:
                fi = li - len(coarse)
                e = jnp.exp(e_fine[fi * c:(fi + 1) * c, hs])
            x = (jnp.where(isq, qh, kh) * e).astype(BF16)
            scores = scores + _dot(x, x, ((1,), (1,))) * mask_ref[li]
        o = _mm(scores, vh)
        st = st_ref[d, h]
        o = o + _mm_nt(qh * jnp.exp(cs[:, hs]), st)
        ks = kh * jnp.exp(tot_log[:, hs] - cs[:, hs])
        st_ref[d, h] = st * jnp.exp(tot_log[:, hs]) + _mm_tn(vh, ks)
        outs.append(o)
    return jnp.concatenate(outs, axis=1)


def _hgrn_kernel(pf_ref, pb_ref, lb_ref, cum_ref, sel_ref, mask_ref, qrow_ref, of_ref, ob_ref, st_ref):
    @pl.when(pl.program_id(1) == 0)
    def _():
        st_ref[...] = jnp.zeros_like(st_ref)

    hd = of_ref.shape[-1]
    lb = lb_ref[...]
    nchunk = pf_ref.shape[0] // HGRN_CHUNK
    for d, (p_ref, o_ref) in enumerate(((pf_ref, of_ref), (pb_ref, ob_ref))):
        order = range(nchunk) if d == 0 else range(nchunk - 1, -1, -1)
        for c in order:
            rows = pl.ds(c * HGRN_CHUNK, HGRN_CHUNK)
            q = p_ref[rows, 0:hd]
            fr = p_ref[rows, (1 + d) * hd:(2 + d) * hd]
            v = p_ref[rows, 3 * hd:4 * hd]
            ff = lb + (1.0 - lb) * _sigmoid(fr)
            o_ref[rows, :] = _hgrn_chunk(q, 1.0 - ff, v, jnp.log(ff), cum_ref[d], sel_ref[d], mask_ref.at[d],
                                         qrow_ref.at[d], st_ref, d)


def _hgrn_scan(p_hgrn, lb, block_t):
    b, L, w = p_hgrn.shape
    hd = w // 4
    nb = L // block_t
    tabs = [_hgrn_tables(rev) for rev in (False, True)]
    cum, sel, mask, qrow = (jnp.asarray(np.stack([t[j] for t in tabs]), dt)
                            for j, dt in enumerate((BF16, BF16, F32, F32)))
    out_sds = jax.ShapeDtypeStruct((b, L, hd), F32)
    return pl.pallas_call(
        _hgrn_kernel,
        grid=(b, nb),
        in_specs=[
            pl.BlockSpec((None, block_t, w), lambda bi, i: (bi, i, 0)),
            pl.BlockSpec((None, block_t, w), lambda bi, i: (bi, nb - 1 - i, 0)),
            _const_spec((1, hd)), _const_spec(cum.shape), _const_spec(sel.shape), _const_spec(mask.shape),
            _const_spec(qrow.shape),
        ],
        out_specs=[
            pl.BlockSpec((None, block_t, hd), lambda bi, i: (bi, i, 0)),
            pl.BlockSpec((None, block_t, hd), lambda bi, i: (bi, nb - 1 - i, 0)),
        ],
        out_shape=[out_sds, out_sds],
        scratch_shapes=[pltpu.VMEM((2, hd // HGRN_HEAD, HGRN_HEAD, HGRN_HEAD), F32)],
        compiler_params=pltpu.CompilerParams(
            dimension_semantics=("arbitrary", "arbitrary"), vmem_limit_bytes=VMEM_LIMIT_BYTES),
        name="hgrn_scan",
    )(p_hgrn, p_hgrn, lb.reshape(1, hd), cum, sel, mask, qrow)


SSD_CHUNK = 128
SSD_HEADS = 16
SSD_HEAD_DIM = 64
SSD_STATE = 128
SSD_GROUPS = 2
SSD_INNER = SSD_HEADS * SSD_HEAD_DIM
SSD_CONV_TAPS = 5
SSD_DT_LANES = 2 * SSD_HEADS
SSD_DT_COPIES = 3
HALO = SUBLANES


def _expand_heads(x, expand):
    hi = x.astype(BF16).astype(F32)
    r1 = x - hi
    mid = r1.astype(BF16).astype(F32)
    lane = lax.broadcasted_iota(jnp.int32, (1, LANES), 1)
    pieces = jnp.where(lane < SSD_DT_LANES, hi, jnp.where(lane < 2 * SSD_DT_LANES, mid, r1 - mid))
    return _dot(pieces.astype(BF16), expand, ((1,), (0,)))


def _ssd_direction(x_ref, dt_ref, dtb_ref, a_ref, tri_ref, exp_ref, dsk_ref, st_ref, o_ref, d):
    t_blk = dt_ref.shape[0]
    gn = SSD_GROUPS * SSD_STATE
    z = dt_ref[...] + dtb_ref[...]
    dt = jnp.maximum(z, 0.0) + jnp.log(1.0 + jnp.exp(-jnp.abs(z)))
    dta = dt * a_ref[...]
    expand = exp_ref[d]
    nchunk = t_blk // SSD_CHUNK
    hp = SSD_INNER // SSD_GROUPS
    row = lax.broadcasted_iota(jnp.int32, (SSD_CHUNK, SSD_CHUNK), 0)
    col = lax.broadcasted_iota(jnp.int32, (SSD_CHUNK, SSD_CHUNK), 1)
    causal = (row >= col) if d == 0 else (row <= col)
    lane = lax.broadcasted_iota(jnp.int32, (1, LANES), 1)
    lo = (lane < SSD_HEAD_DIM).astype(F32)
    hi = 1.0 - lo
    for c in (range(nchunk) if d == 0 else range(nchunk - 1, -1, -1)):
        rs = slice(c * SSD_CHUNK, (c + 1) * SSD_CHUNK)
        acum = _sel_mm(tri_ref[d], dta[rs])
        edge = acum[SSD_CHUNK - 1:SSD_CHUNK] if d == 0 else acum[0:1]
        acum_t = acum.T
        e_in = jnp.exp(acum)
        e_out = jnp.exp(edge - acum)
        wide = _expand_heads(jnp.concatenate([dt[rs], dt[rs] * e_out, e_in], axis=0), expand)
        xs = x_ref[rs, :SSD_INNER]
        xdt = xs * wide[:SSD_CHUNK]
        xout = xs * wide[SSD_CHUNK:2 * SSD_CHUNK]
        e_in_w = wide[2 * SSD_CHUNK:]
        dec = _expand_heads(jnp.broadcast_to(jnp.exp(edge), (SUBLANES, LANES)), expand)[0:1]
        y = []
        for g in range(SSD_GROUPS):
            bg = x_ref[rs, SSD_INNER + g * SSD_STATE:SSD_INNER + (g + 1) * SSD_STATE]
            cg = x_ref[rs, SSD_INNER + gn + g * SSD_STATE:SSD_INNER + gn + (g + 1) * SSD_STATE]
            cb = _mm_nt(cg, bg)
            gs = slice(g * hp, (g + 1) * hp)
            prev = st_ref[d, :, gs]
            y_off = _mm(cg, prev) * e_in_w[:, gs]
            st_ref[d, :, gs] = prev * dec[:, gs] + _mm_tn(bg, xout[:, gs])
            for pr in range(hp // LANES):
                ms = []
                for hh in range(2):
                    ln = 16 * d + g * (SSD_HEADS // SSD_GROUPS) + 2 * pr + hh
                    diff = acum[:, ln:ln + 1] - acum_t[ln:ln + 1, :]
                    ms.append((cb * jnp.exp(jnp.where(causal, diff, -jnp.inf))).astype(BF16))
                xp = xdt[:, g * hp + pr * LANES:g * hp + (pr + 1) * LANES]
                rhs = jnp.concatenate([xp * lo, xp * hi], axis=0).astype(BF16)
                y.append(_dot(jnp.concatenate(ms, axis=1), rhs, ((1,), (0,)))
                         + y_off[:, pr * LANES:(pr + 1) * LANES])
        yc = jnp.concatenate(y, axis=1)
        o_ref[rs, :] = yc + xs * dsk_ref[...] if d == 0 else yc


def _ssd_kernel(xf_ref, xb_ref, dtf_ref, dtb_ref, bias_ref, a_ref, tri_ref, exp_ref, dsk_ref, of_ref, ob_ref,
                st_ref):
    @pl.when(pl.program_id(1) == 0)
    def _():
        st_ref[...] = jnp.zeros_like(st_ref)

    _ssd_direction(xf_ref, dtf_ref, bias_ref, a_ref.at[0], tri_ref, exp_ref, dsk_ref, st_ref, of_ref, 0)
    _ssd_direction(xb_ref, dtb_ref, bias_ref, a_ref.at[1], tri_ref, exp_ref, dsk_ref, st_ref, ob_ref, 1)


def _ssd_tables():
    c = SSD_CHUNK
    tri = np.stack([np.tril(np.ones((c, c), np.float32)), np.triu(np.ones((c, c), np.float32))])
    expand = np.zeros((2, LANES, SSD_INNER), np.float32)
    for d in range(2):
        for h in range(SSD_HEADS):
            for piece in range(SSD_DT_COPIES):
                expand[d, SSD_DT_LANES * piece + SSD_HEADS * d + h, h * SSD_HEAD_DIM:(h + 1) * SSD_HEAD_DIM] = 1.0
    return np.tile(tri, (1, 1, 3)), expand


def _ssd_scan(xact, dt_pad, dt_bias, a_log, d_skip, block_t):
    b, L, w = xact.shape
    nb = L // block_t
    tri, expand = _ssd_tables()
    pad = LANES - SSD_DT_COPIES * SSD_DT_LANES
    bias = jnp.pad(jnp.tile(dt_bias.reshape(1, -1), (1, SSD_DT_COPIES)), ((0, 0), (0, pad)))
    a = -jnp.exp(a_log)
    zero = jnp.zeros_like(a[0])
    a_dir = jnp.stack([jnp.concatenate([a[0], zero]), jnp.concatenate([zero, a[1]])])
    a_pad = jnp.pad(jnp.tile(a_dir, (1, SSD_DT_COPIES)), ((0, 0), (0, pad))).reshape(2, 1, LANES)
    dsk = jnp.repeat(d_skip, SSD_HEAD_DIM).reshape(1, SSD_INNER)
    out_sds = jax.ShapeDtypeStruct((b, L, SSD_INNER), F32)
    fwd = lambda wd: pl.BlockSpec((None, block_t, wd), lambda bi, i: (bi, i, 0))
    bwd = lambda wd: pl.BlockSpec((None, block_t, wd), lambda bi, i: (bi, nb - 1 - i, 0))
    return pl.pallas_call(
        _ssd_kernel,
        grid=(b, nb),
        in_specs=[fwd(w), bwd(w), fwd(LANES), bwd(LANES), _const_spec((1, LANES)), _const_spec((2, 1, LANES)),
                  _const_spec(tri.shape), _const_spec(expand.shape), _const_spec((1, SSD_INNER))],
        out_specs=[
            pl.BlockSpec((None, block_t, SSD_INNER), lambda bi, i: (bi, i, 0)),
            pl.BlockSpec((None, block_t, SSD_INNER), lambda bi, i: (bi, nb - 1 - i, 0)),
        ],
        out_shape=[out_sds, out_sds],
        scratch_shapes=[pltpu.VMEM((2, SSD_STATE, SSD_INNER), F32)],
        compiler_params=pltpu.CompilerParams(
            dimension_semantics=("arbitrary", "arbitrary"), vmem_limit_bytes=VMEM_LIMIT_BYTES),
        name="ssd_scan",
    )(xact, xact, dt_pad, dt_pad, bias, a_pad, jnp.asarray(tri, BF16), jnp.asarray(expand, BF16), dsk)


RWKV_HEAD = 64
RWKV_CHUNK = 64
RWKV_DIM = 512
RWKV_DOUBLINGS = 5


def _bd(y, lo):
    yb = y.astype(BF16)
    zero = jnp.zeros_like(yb)
    return jnp.concatenate([jnp.where(lo, yb, zero), jnp.where(lo, zero, yb)], axis=0)


def _seg_sum(x, seg):
    return jnp.concatenate(
        [_mm_sel(x[:, j * LANES:(j + 1) * LANES], seg) for j in range(x.shape[1] // LANES)], axis=1)


def _rwkv_local(ps_ref, t_blk, w0_ref, wup_ref, a0_ref, aup_ref, gup_ref, kk_ref, ka_ref, rk_ref, seg_ref,
                tri_ref, g_ref, bonus_ref, d):
    dm = RWKV_DIM
    ps = ps_ref[...]
    r, k0, v = ps[:, 0:dm], ps[:, dm:2 * dm], ps[:, 2 * dm:3 * dm]
    wa = ps[:, 3 * dm:3 * dm + LANES]
    lw = -RWKV_DECAY_SCALE * _sigmoid(w0_ref[d] + _mm(jnp.tanh(wa), wup_ref[d]))
    a = _sigmoid(a0_ref[...] + _mm(wa, aup_ref[...]))
    seg = seg_ref[...]
    kk0 = k0 * kk_ref[...]
    kk = kk0 * lax.rsqrt(_seg_sum(kk0 * kk0, seg) + 1e-12)
    k = k0 * (1.0 + (a - 1.0) * ka_ref[...])
    if d == 0:
        g_ref[...] = _mm(_sigmoid(ps[:, 3 * dm + LANES:3 * dm + 2 * LANES]), gup_ref[...])
        bonus_ref[...] = _seg_sum(r * k * rk_ref[...], seg) * v

    gc = _sel_mm2_wide(tri_ref[d], lw)
    en = jnp.exp(-gc)
    rb = r * jnp.exp(gc)
    ab = -(kk * jnp.exp(gc - lw))
    kb = k * en
    bb = (kk * a) * en

    c = RWKV_CHUNK
    tt = lax.broadcasted_iota(jnp.int32, (c, LANES), 0)
    ss = lax.broadcasted_iota(jnp.int32, (c, LANES), 1) & (RWKV_HEAD - 1)
    strict = ((tt > ss) if d == 0 else (tt < ss)).astype(F32)
    eye = (tt == ss).astype(F32)
    incl = strict + eye
    r2 = lax.broadcasted_iota(jnp.int32, (LANES, LANES), 0)
    c2 = lax.broadcasted_iota(jnp.int32, (LANES, LANES), 1)
    same_head = ((r2 >= RWKV_HEAD) == (c2 >= RWKV_HEAD)).astype(F32)
    eye2 = (r2 == c2).astype(F32)
    lo = lax.broadcasted_iota(jnp.int32, (1, LANES), 1) < RWKV_HEAD
    lof = lo.astype(F32)
    hif = 1.0 - lof

    nchunk = t_blk // c
    items = [(ci, j) for ci in range(nchunk) for j in range(dm // LANES)]
    cut = lambda x, it: x[it[0] * c:(it[0] + 1) * c, it[1] * LANES:(it[1] + 1) * LANES]
    am = [cut(ab, it) for it in items]
    rm = [cut(rb, it) for it in items]
    km = [cut(kb, it) for it in items]
    bm = [cut(bb, it) for it in items]
    vm = [cut(v, it) for it in items]
    gram = [_mm_nt(jnp.concatenate([a_, r_], axis=0),
                   jnp.concatenate([k_ * lof, k_ * hif, b_ * lof, b_ * hif], axis=0))
            for a_, r_, k_, b_ in zip(am, rm, km, bm)]
    a_ak = [g[:c, :LANES] * strict for g in gram]
    a_rk = [g[c:, :LANES] * incl for g in gram]
    a_rb = [g[c:, LANES:] * incl for g in gram]
    pw = [g[:c, LANES:] * strict for g in gram]
    tm = [eye + p_ for p_ in pw]
    pw = [_dot(p_.astype(BF16), _bd(p_, lo), ((1,), (0,))) for p_ in pw]
    for j in range(1, RWKV_DOUBLINGS + 1):
        if j < RWKV_DOUBLINGS:
            both = [_dot(p_.astype(BF16), jnp.concatenate([_bd(p_, lo), _bd(t_, lo)], axis=1), ((1,), (0,)))
                    for p_, t_ in zip(pw, tm)]
            pw = [x[:, :LANES] for x in both]
            tm = [t_ + x[:, LANES:] for t_, x in zip(tm, both)]
        else:
            tm = [t_ + _dot(p_.astype(BF16), _bd(t_, lo), ((1,), (0,))) for p_, t_ in zip(pw, tm)]
    bdv = [_bd(v_, lo) for v_ in vm]
    rhs0 = [_dot(a_.astype(BF16), b_, ((1,), (0,))) for a_, b_ in zip(a_ak, bdv)]
    au = [_dot(t_.astype(BF16), jnp.concatenate([_bd(a_, lo), _bd(r_, lo)], axis=1), ((1,), (0,)))
          for t_, a_, r_ in zip(tm, am, rhs0)]
    at = [x[:, :LANES] for x in au]
    uv = [x[:, LANES:] for x in au]
    qt = [r_ + _dot(a_.astype(BF16), _bd(t_, lo), ((1,), (0,))) for r_, a_, t_ in zip(rm, a_rb, at)]
    yloc = [_dot(jnp.concatenate([k_, b_], axis=1).astype(BF16),
                 jnp.concatenate([bv_, _bd(u_, lo)], axis=0), ((1,), (0,)))
            for k_, b_, bv_, u_ in zip(a_rk, a_rb, bdv, uv)]
    mt = [eye2 + same_head * _mm_tn(b_, t_) for b_, t_ in zip(bm, at)]
    nt = [same_head * _mm_tn(jnp.concatenate([v_, u_], axis=0), jnp.concatenate([k_, b_], axis=0))
          for v_, u_, k_, b_ in zip(vm, uv, km, bm)]
    gam = []
    for ci in range(nchunk):
        edge = (ci + 1) * c - 1 if d == 0 else ci * c
        gam.append(jnp.exp(gc[edge:edge + 1, :]))
    return dict(yloc=yloc, qt=qt, mt=mt, nt=nt, gam=gam, items=items)


def _rwkv_kernel(pf_ref, pb_ref, w0_ref, wup_ref, a0_ref, aup_ref, gup_ref, kk_ref, ka_ref, rk_ref, seg_ref,
                 tri_ref, of_ref, ob_ref, g_ref, bonus_ref, st_ref):
    @pl.when(pl.program_id(1) == 0)
    def _():
        st_ref[...] = jnp.zeros_like(st_ref)

    params = (w0_ref, wup_ref, a0_ref, aup_ref, gup_ref, kk_ref, ka_ref, rk_ref, seg_ref, tri_ref)
    t_blk = of_ref.shape[0]
    loc = (_rwkv_local(pf_ref, t_blk, *params, g_ref, bonus_ref, 0),
           _rwkv_local(pb_ref, t_blk, *params, g_ref, bonus_ref, 1))

    c = RWKV_CHUNK
    nchunk = t_blk // c
    npair = RWKV_DIM // LANES
    o_refs = (of_ref, ob_ref)
    ht = [[st_ref[d, j] for j in range(npair)] for d in range(2)]
    for step in range(nchunk):
        for d in range(2):
            ci = step if d == 0 else nchunk - 1 - step
            for j in range(npair):
                n = ci * npair + j
                h = ht[d][j]
                o_refs[d][ci * c:(ci + 1) * c, j * LANES:(j + 1) * LANES] = (
                    loc[d]["yloc"][n] + _mm_nt(loc[d]["qt"][n], h))
                ht[d][j] = ((_mm_nt(h, loc[d]["mt"][n]) + loc[d]["nt"][n])
                            * loc[d]["gam"][ci][:, j * LANES:(j + 1) * LANES])
    for d in range(2):
        for j in range(npair):
            st_ref[d, j] = ht[d][j]


def _rwkv_scan(ps, w0, w_up, a0, a_up, g_up, k_k, k_a, r_k, block_t):
    b, L, w = ps.shape
    dm = RWKV_DIM
    nb = L // block_t
    c = RWKV_CHUNK
    t = np.arange(block_t)
    same = (t[:, None] // c) == (t[None, :] // c)
    tri = np.stack([same & (t[None, :] <= t[:, None]), same & (t[None, :] >= t[:, None])]).astype(np.float32)
    l = np.arange(LANES)
    seg = ((l[:, None] // RWKV_HEAD) == (l[None, :] // RWKV_HEAD)).astype(np.float32)
    rank = w_up.shape[1]
    wup_pad = jnp.zeros((2, LANES, dm), F32).at[:, :rank].set(w_up).astype(BF16)
    aup_pad = jnp.zeros((LANES, dm), F32).at[rank:rank + a_up.shape[0]].set(a_up).astype(BF16)
    row = lambda x: x.reshape(1, dm)
    out_sds = jax.ShapeDtypeStruct((b, L, dm), F32)
    fwd = lambda wd: pl.BlockSpec((None, block_t, wd), lambda bi, i: (bi, i, 0))
    bwd = lambda wd: pl.BlockSpec((None, block_t, wd), lambda bi, i: (bi, nb - 1 - i, 0))
    return pl.pallas_call(
        _rwkv_kernel,
        grid=(b, nb),
        in_specs=[fwd(w), bwd(w), _const_spec((2, 1, dm)), _const_spec((2, LANES, dm)), _const_spec((1, dm)),
                  _const_spec((LANES, dm)), _const_spec((g_up.shape[0], dm)), _const_spec((1, dm)),
                  _const_spec((1, dm)), _const_spec((1, dm)), _const_spec((LANES, LANES)), _const_spec(tri.shape)],
        out_specs=[fwd(dm), bwd(dm), fwd(dm), fwd(dm)],
        out_shape=[out_sds] * 4,
        scratch_shapes=[pltpu.VMEM((2, dm // LANES, LANES, LANES), F32)],
        compiler_params=pltpu.CompilerParams(
            dimension_semantics=("arbitrary", "arbitrary"), vmem_limit_bytes=VMEM_LIMIT_BYTES),
        name="rwkv_scan",
    )(ps, ps, w0.reshape(2, 1, dm), wup_pad, row(a0), aup_pad, g_up.astype(BF16), row(k_k), row(k_a), row(r_k),
      jnp.asarray(seg, BF16), jnp.asarray(tri, BF16))


def _rms(x, w):
    return x * lax.rsqrt(jnp.mean(x * x, axis=-1, keepdims=True) + NORM_EPS) * w


def _in_proj_kernel(x_ref, xp_ref, xn_ref, nw_ref, wloc_ref, wpt_ref, cw_ref, cb_ref, mu_ref,
                    xact_ref, ps_ref, dt_ref, ph_ref, *, blocks_per_seq):
    i = pl.program_id(0)
    t = x_ref.shape[0]
    te = t + 2 * HALO
    dims = ((1,), (0,))
    xe = _rms(jnp.concatenate([xp_ref[...], x_ref[...], xn_ref[...]], axis=0), nw_ref[...])
    pe = _dot(xe.astype(BF16), wloc_ref[...], dims)
    pt = _dot(xe[HALO:HALO + t].astype(BF16), wpt_ref[...], dims)
    dt_ref[...] = pt[:, :LANES]
    ph_ref[...] = pt[:, LANES:]

    row = lax.broadcasted_iota(jnp.int32, (te, 1), 0)
    first = (i % blocks_per_seq) == 0
    last = (i % blocks_per_seq) == blocks_per_seq - 1
    outside = jnp.logical_or(jnp.logical_and(first, row < HALO), jnp.logical_and(last, row >= HALO + t))
    pe = jnp.where(outside, 0.0, pe)
    wc = cw_ref.shape[1]
    xb, pr = pe[:, :wc], pe[:, wc:]
    half = (SSD_CONV_TAPS - 1) // 2
    acc = cb_ref[...] + cw_ref[half:half + 1, :] * xb[HALO:HALO + t]
    for j in range(SSD_CONV_TAPS):
        if j != half:
            acc = acc + cw_ref[j:j + 1, :] * pltpu.roll(xb, (half - j) % te, 0)[HALO:HALO + t]
    xact_ref[...] = _silu(acc)
    p = pr[HALO:HALO + t]
    prev = pltpu.roll(pr, 1, 0)[HALO:HALO + t]
    nxt = pltpu.roll(pr, te - 1, 0)[HALO:HALO + t]
    ps_ref[...] = p + mu_ref[0:1, :] * (prev - p) + mu_ref[1:2, :] * (nxt - p)


def _in_proj(x, norm_w, w_conv, w_rwkv, w_dt, w_hgrn, conv_w, conv_b, mu, seq_len, block_t):
    t, dmodel = x.shape
    r = block_t // HALO
    last = t // HALO - 1
    wc, wr = w_conv.shape[1], w_rwkv.shape[1]
    w_loc = jnp.concatenate([w_conv, w_rwkv], axis=1)
    w_pt = jnp.concatenate([w_dt, w_hgrn], axis=1)
    taps = jnp.zeros((SUBLANES, wc), F32).at[:SSD_CONV_TAPS].set(conv_w[:, 0, :])
    tok = lambda wd: pl.BlockSpec((block_t, wd), lambda i: (i, 0))
    widths = (wc, wr, w_dt.shape[1], w_hgrn.shape[1])
    return pl.pallas_call(
        functools.partial(_in_proj_kernel, blocks_per_seq=seq_len // block_t),
        grid=(t // block_t,),
        in_specs=[tok(dmodel),
                  pl.BlockSpec((HALO, dmodel), lambda i: (jnp.maximum(i * r - 1, 0), 0)),
                  pl.BlockSpec((HALO, dmodel), lambda i: (jnp.minimum((i + 1) * r, last), 0)),
                  _const_spec((1, dmodel)), _const_spec(w_loc.shape), _const_spec(w_pt.shape),
                  _const_spec(taps.shape), _const_spec((1, wc)), _const_spec(mu.shape)],
        out_specs=[tok(wd) for wd in widths],
        out_shape=[jax.ShapeDtypeStruct((t, wd), F32) for wd in widths],
        compiler_params=pltpu.CompilerParams(
            dimension_semantics=("arbitrary",), vmem_limit_bytes=VMEM_LIMIT_BYTES),
        name="in_proj",
    )(x, x, x, norm_w.reshape(1, dmodel), w_loc, w_pt, taps, conv_b.reshape(1, wc), mu)


def _merge_kernel(x_ref, sf_ref, sb_ref, rf_ref, rb_ref, rg_ref, rbon_ref, hf_ref, hb_ref,
                  nw1_ref, wz_ref, whg_ref, wgate_ref, snw_ref, lnw_ref, lnb_ref, hnw_ref, seg_ref,
                  wbs_ref, wbr_ref, wbh_ref, wout_ref, o_ref):
    dmodel = x_ref.shape[1]
    dims = ((1,), (0,))
    xn = _rms(x_ref[...], nw1_ref[...]).astype(BF16)
    ys = (sf_ref[...] + sb_ref[...]) * _silu(_dot(xn, wz_ref[...], dims))
    y1 = _mm(_rms(ys, snw_ref[...]), wbs_ref[...])

    seg = seg_ref[...]
    yr = rf_ref[...] + rb_ref[...]
    cen = yr - _seg_sum(yr, seg) * (1.0 / RWKV_HEAD)
    var = _seg_sum(cen * cen, seg) * (1.0 / RWKV_HEAD)
    yn = cen * lax.rsqrt(var + RWKV_LN_EPS) * lnw_ref[...] + lnb_ref[...]
    y2 = _mm((yn + rbon_ref[...]) * rg_ref[...], wbr_ref[...])

    yh = hf_ref[...] + hb_ref[...]
    yh = jnp.concatenate(
        [_rms(yh[:, j * HGRN_HEAD:(j + 1) * HGRN_HEAD], hnw_ref[:, j * HGRN_HEAD:(j + 1) * HGRN_HEAD])
         for j in range(yh.shape[1] // HGRN_HEAD)], axis=1)
    y3 = _mm(yh * _silu(_dot(xn, whg_ref[...], dims)), wbh_ref[...])

    gates = _sigmoid(_dot(xn, wgate_ref[...], dims))
    mixed = (gates[:, 0:dmodel] * y1 + gates[:, dmodel:2 * dmodel] * y2 + gates[:, 2 * dmodel:3 * dmodel] * y3)
    o_ref[...] = x_ref[...] + _mm(mixed, wout_ref[...])


def _merge(x, sf, sb, rf, rb, rg, rbon, hf, hb, norm1_w, w_z, w_hg, w_gate, ssm_norm_w, ln_w, ln_b,
           hgrn_norm_w, wb_ssm, wb_rwkv, wb_hgrn, w_out, block_t):
    t, dmodel = x.shape
    hd = hf.shape[1]
    l = np.arange(LANES)
    seg = jnp.asarray((l[:, None] // RWKV_HEAD) == (l[None, :] // RWKV_HEAD), BF16)
    tok = lambda w: pl.BlockSpec((block_t, w), lambda i: (i, 0))
    full = lambda a: _const_spec(a.shape)
    params = [norm1_w.reshape(1, -1), w_z, w_hg, w_gate, ssm_norm_w.reshape(1, -1), ln_w.reshape(1, -1),
              ln_b.reshape(1, -1), hgrn_norm_w.reshape(1, -1), seg, wb_ssm, wb_rwkv, wb_hgrn, w_out]
    return pl.pallas_call(
        _merge_kernel,
        grid=(t // block_t,),
        in_specs=[tok(dmodel), tok(dmodel), tok(dmodel), tok(hd), tok(hd), tok(hd), tok(hd), tok(hd), tok(hd)]
        + [full(a) for a in params],
        out_specs=tok(dmodel),
        out_shape=jax.ShapeDtypeStruct((t, dmodel), F32),
        compiler_params=pltpu.CompilerParams(
            dimension_semantics=("arbitrary",), vmem_limit_bytes=VMEM_LIMIT_BYTES),
        name="merge",
    )(x, sf, sb, rf, rb, rg, rbon, hf, hb, *params)


FFN_SPLIT = 2


def _ffn_kernel(x_ref, nw_ref, win_ref, wdown_ref, fw_ref, o_ref, *, final):
    x = x_ref[...]
    xn = _rms(x, nw_ref[...]).astype(BF16)
    dff = wdown_ref.shape[0]
    step = dff // FFN_SPLIT
    acc = x
    for c in range(FFN_SPLIT):
        hg = _dot(xn, win_ref[:, c * step:(c + 1) * step], ((1,), (0,)))
        hu = _dot(xn, win_ref[:, dff + c * step:dff + (c + 1) * step], ((1,), (0,)))
        acc = acc + _mm(_silu(hg) * hu, wdown_ref[c * step:(c + 1) * step, :])
    o_ref[...] = _rms(acc, fw_ref[...]) if final else acc


def _ffn(x, norm_w, w_in, w_down, final_w, final, block_t):
    t, dmodel = x.shape
    tok = pl.BlockSpec((block_t, dmodel), lambda i: (i, 0))
    return pl.pallas_call(
        functools.partial(_ffn_kernel, final=final),
        grid=(t // block_t,),
        in_specs=[tok, _const_spec((1, dmodel)), _const_spec(w_in.shape), _const_spec(w_down.shape),
                  _const_spec((1, dmodel))],
        out_specs=tok,
        out_shape=jax.ShapeDtypeStruct((t, dmodel), F32),
        compiler_params=pltpu.CompilerParams(
            dimension_semantics=("arbitrary",), vmem_limit_bytes=VMEM_LIMIT_BYTES),
        name="ffn",
    )(x, norm_w.reshape(1, dmodel), w_in, w_down, final_w.reshape(1, dmodel))


SCAN_BLOCK_T = 256
DENSE_BLOCK_T = 256
FFN_BLOCK_T = 512


def kernel(x, norm1_w, w_in, ssm_conv_w, ssm_conv_b, ssm_dt_bias, ssm_a_log, ssm_d, ssm_norm_w, rwkv_mu, rwkv_w0, rwkv_w_up, rwkv_a0, rwkv_a_up, rwkv_g_up, rwkv_k_k, rwkv_k_a, rwkv_r_k, rwkv_ln_w, rwkv_ln_b, hgrn_lb_logits, hgrn_norm_w, w_branch_ssm, w_branch_rwkv, w_branch_hgrn, w_out, norm2_w, ffn_w_in, ffn_w_down, final_norm_w):
    b, L, dmodel = x.shape
    depth = w_in.shape[0]
    lb_p = jax.nn.softmax(hgrn_lb_logits.astype(F32), axis=0)
    lower_bounds = jnp.cumsum(lb_p, axis=0) - lb_p[0]
    conv_dim = ssm_conv_w.shape[-1]
    n_dt = ssm_dt_bias.shape[1] * ssm_dt_bias.shape[2]
    rw, hw = rwkv_mu.shape[-1], 5 * hgrn_norm_w.shape[-1]
    bounds = np.cumsum([0, SSD_INNER, conv_dim, n_dt, rw, hw, 3 * dmodel])
    seq = lambda a: a.reshape(b, L, a.shape[-1])
    xt = x.reshape(b * L, dmodel)
    for l in range(depth):
        w = w_in[l].astype(BF16)
        wz, wxbc, wdt, wrw, whg, wgate = (w[:, bounds[i]:bounds[i + 1]] for i in range(6))
        wdt = jnp.pad(jnp.tile(wdt, (1, SSD_DT_COPIES)), ((0, 0), (0, LANES - SSD_DT_COPIES * n_dt)))
        hd = hw // 5
        xact, ps, dtp, p_hgrn = _in_proj(xt, norm1_w[l], wxbc, wrw, wdt, whg[:, :4 * hd], ssm_conv_w[l],
                                         ssm_conv_b[l], rwkv_mu[l], L, DENSE_BLOCK_T)
        sf, sb = _ssd_scan(seq(xact), seq(dtp), ssm_dt_bias[l], ssm_a_log[l], ssm_d[l], SCAN_BLOCK_T)
        rf, rb, rg, rbon = _rwkv_scan(seq(ps), rwkv_w0[l], rwkv_w_up[l], rwkv_a0[l], rwkv_a_up[l],
                                      rwkv_g_up[l], rwkv_k_k[l], rwkv_k_a[l], rwkv_r_k[l], SCAN_BLOCK_T)
        hf, hb = _hgrn_scan(seq(p_hgrn), lower_bounds[l], SCAN_BLOCK_T)
        flat = lambda a: a.reshape(b * L, a.shape[-1])
        xt = _merge(xt, flat(sf), flat(sb), flat(rf), flat(rb), flat(rg), flat(rbon), flat(hf), flat(hb),
                    norm1_w[l], wz, whg[:, 4 * hd:], wgate, ssm_norm_w[l], rwkv_ln_w[l], rwkv_ln_b[l], hgrn_norm_w[l],
                    w_branch_ssm[l].astype(BF16), w_branch_rwkv[l].astype(BF16), w_branch_hgrn[l].astype(BF16),
                    w_out[l].astype(BF16), DENSE_BLOCK_T)
        xt = _ffn(xt, norm2_w[l], ffn_w_in[l].astype(BF16), ffn_w_down[l].astype(BF16), final_norm_w,
                  l == depth - 1, FFN_BLOCK_T)
    return xt.reshape(b, L, dmodel)
```

```python
import functools

import numpy as np
import jax
import jax.numpy as jnp
from jax import lax
from jax.experimental import pallas as pl
from jax.experimental.pallas import tpu as pltpu

F32 = jnp.float32
BF16 = jnp.bfloat16

LANES = 128
SUBLANES = 8
VMEM_LIMIT_BYTES = 56 * 1024 * 1024

NORM_EPS = 1e-6
RWKV_DECAY_SCALE = 0.6065306597
RWKV_LN_EPS = 64e-5

HGRN_HEAD = 128
HGRN_CHUNK = 128
HGRN_LEVELS = (128, 64, 32, 16, 8, 4, 2)


def _dot(a, b, dims):
    return lax.dot_general(a, b, (dims, ((), ())), preferred_element_type=F32)


def _mm(a, b):
    return _dot(a.astype(BF16), b.astype(BF16), ((1,), (0,)))


def _mm_nt(a, b):
    return _dot(a.astype(BF16), b.astype(BF16), ((1,), (1,)))


def _mm_tn(a, b):
    return _dot(a.astype(BF16), b.astype(BF16), ((0,), (0,)))


def _split3(x):
    hi = x.astype(BF16)
    r1 = x - hi.astype(F32)
    mid = r1.astype(BF16)
    lo = (r1 - mid.astype(F32)).astype(BF16)
    return hi, mid, lo


def _sel_mm(sel3, x):
    return _dot(sel3, jnp.concatenate(_split3(x), axis=0), ((1,), (0,)))


def _sel_mm2(sel2, x):
    hi = x.astype(BF16)
    lo = (x - hi.astype(F32)).astype(BF16)
    return _dot(sel2, jnp.concatenate([hi, lo], axis=0), ((1,), (0,)))


def _sel_mm2_wide(sel, x):
    hi = x.astype(BF16)
    lo = (x - hi.astype(F32)).astype(BF16)
    dims = ((1,), (0,))
    return _dot(sel, hi, dims) + _dot(sel, lo, dims)


def _const_spec(shape):
    zeros = (0,) * len(shape)
    return pl.BlockSpec(shape, lambda *_: zeros, pipeline_mode=pl.Buffered(1))


def _sigmoid(x):
    return 1.0 / (1.0 + jnp.exp(-x))


def _silu(x):
    return x * _sigmoid(x)


def _hgrn_tables(reverse):
    c = HGRN_CHUNK
    nl = len(HGRN_LEVELS)
    fine = [n for n in HGRN_LEVELS if n <= SUBLANES]
    cum = np.triu(np.ones((c, c), np.float32)) if reverse else np.tril(np.ones((c, c), np.float32))
    sel = np.zeros((len(fine) * c, c), np.float32)
    mask = np.zeros((nl + 1, c, c), np.float32)
    qrow = np.zeros((nl, c, HGRN_HEAD), np.float32)
    t = np.arange(c)
    for li, n in enumerate(HGRN_LEVELS):
        mid = t // n * n + n // 2
        qrow[li, (t < mid) if reverse else (t >= mid)] = 1.0
        if n in fine:
            fi = fine.index(n)
            for i in range(c):
                m = mid[i]
                if not reverse:
                    sel[fi * c + i, (m if i >= m else i + 1):(i + 1 if i >= m else m)] = 1.0
                else:
                    sel[fi * c + i, (i if i < m else m):(m if i < m else i)] = 1.0
        same = (t[:, None] // n) == (t[None, :] // n)
        isq = qrow[li, :, 0] > 0.5
        mask[li] = same & isq[:, None] & (~isq)[None, :]
    mask[nl] = np.eye(c)
    return np.tile(cum, (1, 3)), np.tile(sel, (1, 2)), mask, qrow


def _hgrn_chunk(q, k, v, logf, cum, sel, mask_ref, qrow_ref, st_ref, d):
    c = HGRN_CHUNK
    nl = len(HGRN_LEVELS)
    coarse = [n for n in HGRN_LEVELS if n > SUBLANES]
    cs = _sel_mm(cum, logf)
    e_fine = _sel_mm2(sel, logf)
    d_coarse = []
    for n in coarse:
        ref = jnp.concatenate(
            [jnp.broadcast_to(cs[r:r + 1, :], (n, cs.shape[1]))
             for r in range(n // 2 - (1 - d), c, n)], axis=0)
        d_coarse.append(cs - ref)
    edge = c - 1 if d == 0 else 0
    tot_log = cs[edge:edge + 1, :]
    outs = []
    for h in range(q.shape[1] // HGRN_HEAD):
        hs = slice(h * HGRN_HEAD, (h + 1) * HGRN_HEAD)
        qh, kh, vh = q[:, hs], k[:, hs], v[:, hs]
        scores = _mm_nt(qh, kh) * mask_ref[nl]
        for li in range(nl):
            isq = qrow_ref[li] > 0.5
            if li < len(coarse):
                dd = d_coarse[li][:, hs]
                e = jnp.exp(jnp.where(isq, dd, -dd))
            else:
                fi = li - len(coarse)
                e = jnp.exp(e_fine[fi * c:(fi + 1) * c, hs])
            x = (jnp.where(isq, qh, kh) * e).astype(BF16)
            scores = scores + _dot(x, x, ((1,), (1,))) * mask_ref[li]
        o = _mm(scores, vh)
        st = st_ref[d, h]
        o = o + _mm_nt(qh * jnp.exp(cs[:, hs]), st)
        ks = kh * jnp.exp(tot_log[:, hs] - cs[:, hs])
        st_ref[d, h] = st * jnp.exp(tot_log[:, hs]) + _mm_tn(vh, ks)
        outs.append(o)
        yield
    return jnp.concatenate(outs, axis=1)


def _hgrn_body(pf_ref, pb_ref, lb_ref, cum_ref, sel_ref, mask_ref, qrow_ref, of_ref, ob_ref, st_ref):
    hd = of_ref.shape[-1]
    lb = lb_ref[...]
    nchunk = pf_ref.shape[0] // HGRN_CHUNK
    for d, (p_ref, o_ref) in enumerate(((pf_ref, of_ref), (pb_ref, ob_ref))):
        order = range(nchunk) if d == 0 else range(nchunk - 1, -1, -1)
        for c in order:
            rows = pl.ds(c * HGRN_CHUNK, HGRN_CHUNK)
            q = p_ref[rows, 0:hd]
            fr = p_ref[rows, (1 + d) * hd:(2 + d) * hd]
            v = p_ref[rows, 3 * hd:4 * hd]
            ff = lb + (1.0 - lb) * _sigmoid(fr)
            o_ref[rows, :] = yield from _hgrn_chunk(q, 1.0 - ff, v, jnp.log(ff), cum_ref[d], sel_ref[d],
                                                    mask_ref.at[d], qrow_ref.at[d], st_ref, d)


def _fwd_spec(block_t, w):
    return pl.BlockSpec((None, block_t, w), lambda bi, i: (bi, i, 0))


def _bwd_spec(block_t, w, nb):
    return pl.BlockSpec((None, block_t, w), lambda bi, i: (bi, nb - 1 - i, 0))


def _hgrn_operands(p_hgrn, lb, block_t, nb):
    w = p_hgrn.shape[-1]
    hd = w // 4
    tabs = [_hgrn_tables(rev) for rev in (False, True)]
    cum, sel, mask, qrow = (jnp.asarray(np.stack([t[j] for t in tabs]), dt)
                            for j, dt in enumerate((BF16, BF16, F32, F32)))
    args = [p_hgrn, p_hgrn, lb.reshape(1, hd), cum, sel, mask, qrow]
    specs = [_fwd_spec(block_t, w), _bwd_spec(block_t, w, nb)] + [_const_spec(a.shape) for a in args[2:]]
    return args, specs, pltpu.VMEM((2, hd // HGRN_HEAD, HGRN_HEAD, HGRN_HEAD), F32)


SSD_CHUNK = 128
SSD_HEADS = 16
SSD_HEAD_DIM = 64
SSD_STATE = 128
SSD_GROUPS = 2
SSD_INNER = SSD_HEADS * SSD_HEAD_DIM
SSD_CONV_TAPS = 5
SSD_DT_LANES = 2 * SSD_HEADS
SSD_DT_COPIES = 3
HALO = SUBLANES


def _expand_heads(x, expand):
    hi = x.astype(BF16).astype(F32)
    r1 = x - hi
    mid = r1.astype(BF16).astype(F32)
    lane = lax.broadcasted_iota(jnp.int32, (1, LANES), 1)
    pieces = jnp.where(lane < SSD_DT_LANES, hi, jnp.where(lane < 2 * SSD_DT_LANES, mid, r1 - mid))
    return _dot(pieces.astype(BF16), expand, ((1,), (0,)))


def _ssd_direction(x_ref, dt_ref, dtb_ref, a_ref, tri_ref, exp_ref, dsk_ref, st_ref, o_ref, d):
    t_blk = dt_ref.shape[0]
    gn = SSD_GROUPS * SSD_STATE
    z = dt_ref[...] + dtb_ref[...]
    dt = jnp.maximum(z, 0.0) + jnp.log(1.0 + jnp.exp(-jnp.abs(z)))
    dta = dt * a_ref[...]
    expand = exp_ref[d]
    nchunk = t_blk // SSD_CHUNK
    hp = SSD_INNER // SSD_GROUPS
    row = lax.broadcasted_iota(jnp.int32, (SSD_CHUNK, SSD_CHUNK), 0)
    col = lax.broadcasted_iota(jnp.int32, (SSD_CHUNK, SSD_CHUNK), 1)
    causal = (row >= col) if d == 0 else (row <= col)
    lane = lax.broadcasted_iota(jnp.int32, (1, LANES), 1)
    lo = (lane < SSD_HEAD_DIM).astype(F32)
    hi = 1.0 - lo
    for c in (range(nchunk) if d == 0 else range(nchunk - 1, -1, -1)):
        rs = slice(c * SSD_CHUNK, (c + 1) * SSD_CHUNK)
        acum = _sel_mm(tri_ref[d], dta[rs])
        edge = acum[SSD_CHUNK - 1:SSD_CHUNK] if d == 0 else acum[0:1]
        acum_t = acum.T
        e_in = jnp.exp(acum)
        e_out = jnp.exp(edge - acum)
        wide = _expand_heads(jnp.concatenate([dt[rs], dt[rs] * e_out, e_in], axis=0), expand)
        xs = x_ref[rs, :SSD_INNER]
        xdt = xs * wide[:SSD_CHUNK]
        xout = xs * wide[SSD_CHUNK:2 * SSD_CHUNK]
        e_in_w = wide[2 * SSD_CHUNK:]
        dec = _expand_heads(jnp.broadcast_to(jnp.exp(edge), (SUBLANES, LANES)), expand)[0:1]
        y = []
        for g in range(SSD_GROUPS):
            bg = x_ref[rs, SSD_INNER + g * SSD_STATE:SSD_INNER + (g + 1) * SSD_STATE]
            cg = x_ref[rs, SSD_INNER + gn + g * SSD_STATE:SSD_INNER + gn + (g + 1) * SSD_STATE]
            cb = _mm_nt(cg, bg)
            gs = slice(g * hp, (g + 1) * hp)
            prev = st_ref[d, :, gs]
            y_off = _mm(cg, prev) * e_in_w[:, gs]
            st_ref[d, :, gs] = prev * dec[:, gs] + _mm_tn(bg, xout[:, gs])
            for pr in range(hp // LANES):
                ms = []
                for hh in range(2):
                    ln = 16 * d + g * (SSD_HEADS // SSD_GROUPS) + 2 * pr + hh
                    diff = acum[:, ln:ln + 1] - acum_t[ln:ln + 1, :]
                    ms.append((cb * jnp.exp(jnp.where(causal, diff, -jnp.inf))).astype(BF16))
                xp = xdt[:, g * hp + pr * LANES:g * hp + (pr + 1) * LANES]
                rhs = jnp.concatenate([xp * lo, xp * hi], axis=0).astype(BF16)
                y.append(_dot(jnp.concatenate(ms, axis=1), rhs, ((1,), (0,)))
                         + y_off[:, pr * LANES:(pr + 1) * LANES])
                if pr % 2 == 1:
                    yield
        yc = jnp.concatenate(y, axis=1)
        o_ref[rs, :] = yc + xs * dsk_ref[...] if d == 0 else yc
        yield


def _ssd_body(xf_ref, xb_ref, dtf_ref, dtb_ref, bias_ref, a_ref, tri_ref, exp_ref, dsk_ref, of_ref, ob_ref,
              st_ref):
    yield from _ssd_direction(xf_ref, dtf_ref, bias_ref, a_ref.at[0], tri_ref, exp_ref, dsk_ref, st_ref, of_ref, 0)
    yield from _ssd_direction(xb_ref, dtb_ref, bias_ref, a_ref.at[1], tri_ref, exp_ref, dsk_ref, st_ref, ob_ref, 1)


def _ssd_tables():
    c = SSD_CHUNK
    tri = np.stack([np.tril(np.ones((c, c), np.float32)), np.triu(np.ones((c, c), np.float32))])
    expand = np.zeros((2, LANES, SSD_INNER), np.float32)
    for d in range(2):
        for h in range(SSD_HEADS):
            for piece in range(SSD_DT_COPIES):
                expand[d, SSD_DT_LANES * piece + SSD_HEADS * d + h, h * SSD_HEAD_DIM:(h + 1) * SSD_HEAD_DIM] = 1.0
    return np.tile(tri, (1, 1, 3)), expand


def _ssd_operands(xact, dt_pad, dt_bias, a_log, d_skip, block_t, nb):
    w = xact.shape[-1]
    tri, expand = _ssd_tables()
    pad = LANES - SSD_DT_COPIES * SSD_DT_LANES
    bias = jnp.pad(jnp.tile(dt_bias.reshape(1, -1), (1, SSD_DT_COPIES)), ((0, 0), (0, pad)))
    a = -jnp.exp(a_log)
    zero = jnp.zeros_like(a[0])
    a_dir = jnp.stack([jnp.concatenate([a[0], zero]), jnp.concatenate([zero, a[1]])])
    a_pad = jnp.pad(jnp.tile(a_dir, (1, SSD_DT_COPIES)), ((0, 0), (0, pad))).reshape(2, 1, LANES)
    dsk = jnp.repeat(d_skip, SSD_HEAD_DIM).reshape(1, SSD_INNER)
    args = [xact, xact, dt_pad, dt_pad, bias, a_pad, jnp.asarray(tri, BF16), jnp.asarray(expand, BF16), dsk]
    specs = ([_fwd_spec(block_t, w), _bwd_spec(block_t, w, nb), _fwd_spec(block_t, LANES),
              _bwd_spec(block_t, LANES, nb)] + [_const_spec(a.shape) for a in args[4:]])
    return args, specs, pltpu.VMEM((2, SSD_STATE, SSD_INNER), F32)


RWKV_HEAD = 64
RWKV_CHUNK = 64
RWKV_DIM = 512
RWKV_DOUBLINGS = 5


def _bd(y, lo):
    yb = y.astype(BF16)
    zero = jnp.zeros_like(yb)
    return jnp.concatenate([jnp.where(lo, yb, zero), jnp.where(lo, zero, yb)], axis=0)


def _seg_sum(x, seg2):
    outs = []
    for j in range(x.shape[1] // LANES):
        xj = x[:, j * LANES:(j + 1) * LANES]
        hi = xj.astype(BF16)
        lo = (xj - hi.astype(F32)).astype(BF16)
        outs.append(_dot(jnp.concatenate([hi, lo], axis=1), seg2, ((1,), (0,))))
    return jnp.concatenate(outs, axis=1)


def _rwkv_local(ps_ref, t_blk, w0_ref, wup_ref, a0_ref, aup_ref, gup_ref, kk_ref, ka_ref, rk_ref, seg_ref,
                tri_ref, g_ref, bonus_ref, d):
    dm = RWKV_DIM
    ps = ps_ref[...]
    r, k0, v = ps[:, 0:dm], ps[:, dm:2 * dm], ps[:, 2 * dm:3 * dm]
    wa = ps[:, 3 * dm:3 * dm + LANES]
    lw = -RWKV_DECAY_SCALE * _sigmoid(w0_ref[d] + _mm(jnp.tanh(wa), wup_ref[d]))
    a = _sigmoid(a0_ref[...] + _mm(wa, aup_ref[...]))
    yield
    seg = seg_ref[...]
    kk0 = k0 * kk_ref[...]
    kk = kk0 * lax.rsqrt(_seg_sum(kk0 * kk0, seg) + 1e-12)
    k = k0 * (1.0 + (a - 1.0) * ka_ref[...])
    yield
    if d == 0:
        g_ref[...] = _mm(_sigmoid(ps[:, 3 * dm + LANES:3 * dm + 2 * LANES]), gup_ref[...])
        bonus_ref[...] = _seg_sum(r * k * rk_ref[...], seg) * v
        yield

    gc = _sel_mm2_wide(tri_ref[d], lw)
    en = jnp.exp(-gc)
    rb = r * jnp.exp(gc)
    ab = -(kk * jnp.exp(gc - lw))
    kb = k * en
    bb = (kk * a) * en
    yield

    c = RWKV_CHUNK
    tt = lax.broadcasted_iota(jnp.int32, (c, LANES), 0)
    ss = lax.broadcasted_iota(jnp.int32, (c, LANES), 1) & (RWKV_HEAD - 1)
    strict = ((tt > ss) if d == 0 else (tt < ss)).astype(F32)
    eye = (tt == ss).astype(F32)
    incl = strict + eye
    r2 = lax.broadcasted_iota(jnp.int32, (LANES, LANES), 0)
    c2 = lax.broadcasted_iota(jnp.int32, (LANES, LANES), 1)
    same_head = ((r2 >= RWKV_HEAD) == (c2 >= RWKV_HEAD)).astype(F32)
    eye2 = (r2 == c2).astype(F32)
    lo = lax.broadcasted_iota(jnp.int32, (1, LANES), 1) < RWKV_HEAD
    lof = lo.astype(F32)
    hif = 1.0 - lof

    nchunk = t_blk // c
    items = [(ci, j) for ci in range(nchunk) for j in range(dm // LANES)]
    cut = lambda x, it: x[it[0] * c:(it[0] + 1) * c, it[1] * LANES:(it[1] + 1) * LANES]
    am = [cut(ab, it) for it in items]
    rm = [cut(rb, it) for it in items]
    km = [cut(kb, it) for it in items]
    bm = [cut(bb, it) for it in items]
    vm = [cut(v, it) for it in items]
    nn = ((1,), (0,))
    gram = yield from _each(
        lambda a_, r_, k_, b_: _mm_nt(jnp.concatenate([a_, r_], axis=0),
                                      jnp.concatenate([k_ * lof, k_ * hif, b_ * lof, b_ * hif], axis=0)),
        am, rm, km, bm)
    a_ak = [g[:c, :LANES] * strict for g in gram]
    a_rk = [g[c:, :LANES] * incl for g in gram]
    a_rb = [g[c:, LANES:] * incl for g in gram]
    pw = [g[:c, LANES:] * strict for g in gram]
    tm = [eye + p_ for p_ in pw]
    pw = yield from _each(lambda p_: _dot(p_.astype(BF16), _bd(p_, lo), nn), pw)
    for j in range(1, RWKV_DOUBLINGS + 1):
        if j < RWKV_DOUBLINGS:
            both = yield from _each(
                lambda p_, t_: _dot(p_.astype(BF16), jnp.concatenate([_bd(p_, lo), _bd(t_, lo)], axis=1), nn),
                pw, tm)
            pw = [x[:, :LANES] for x in both]
            tm = [t_ + x[:, LANES:] for t_, x in zip(tm, both)]
        else:
            tm = yield from _each(lambda p_, t_: t_ + _dot(p_.astype(BF16), _bd(t_, lo), nn), pw, tm)
    bdv = [_bd(v_, lo) for v_ in vm]
    rhs0 = yield from _each(lambda a_, b_: _dot(a_.astype(BF16), b_, nn), a_ak, bdv)
    au = yield from _each(
        lambda t_, a_, r_: _dot(t_.astype(BF16), jnp.concatenate([_bd(a_, lo), _bd(r_, lo)], axis=1), nn),
        tm, am, rhs0)
    at = [x[:, :LANES] for x in au]
    uv = [x[:, LANES:] for x in au]
    qt = yield from _each(lambda r_, a_, t_: r_ + _dot(a_.astype(BF16), _bd(t_, lo), nn), rm, a_rb, at)
    yloc = yield from _each(
        lambda k_, b_, bv_, u_: _dot(jnp.concatenate([k_, b_], axis=1).astype(BF16),
                                     jnp.concatenate([bv_, _bd(u_, lo)], axis=0), nn),
        a_rk, a_rb, bdv, uv)
    mt = yield from _each(lambda b_, t_: eye2 + same_head * _mm_tn(b_, t_), bm, at)
    nt = yield from _each(
        lambda v_, u_, k_, b_: same_head * _mm_tn(jnp.concatenate([v_, u_], axis=0),
                                                  jnp.concatenate([k_, b_], axis=0)),
        vm, uv, km, bm)
    gam = []
    for ci in range(nchunk):
        edge = (ci + 1) * c - 1 if d == 0 else ci * c
        gam.append(jnp.exp(gc[edge:edge + 1, :]))
    return dict(yloc=yloc, qt=qt, mt=mt, nt=nt, gam=gam, items=items)


def _each(fn, *lists, every=4):
    out = []
    for n, args in enumerate(zip(*lists)):
        out.append(fn(*args))
        if n % every == every - 1:
            yield
    return out


def _rwkv_body(pf_ref, pb_ref, w0_ref, wup_ref, a0_ref, aup_ref, gup_ref, kk_ref, ka_ref, rk_ref, seg_ref,
               tri_ref, of_ref, ob_ref, g_ref, bonus_ref, st_ref):
    params = (w0_ref, wup_ref, a0_ref, aup_ref, gup_ref, kk_ref, ka_ref, rk_ref, seg_ref, tri_ref)
    t_blk = of_ref.shape[0]
    loc0 = yield from _rwkv_local(pf_ref, t_blk, *params, g_ref, bonus_ref, 0)
    loc1 = yield from _rwkv_local(pb_ref, t_blk, *params, g_ref, bonus_ref, 1)
    loc = (loc0, loc1)

    c = RWKV_CHUNK
    nchunk = t_blk // c
    npair = RWKV_DIM // LANES
    o_refs = (of_ref, ob_ref)
    ht = [[st_ref[d, j] for j in range(npair)] for d in range(2)]
    for step in range(nchunk):
        for d in range(2):
            ci = step if d == 0 else nchunk - 1 - step
            for j in range(npair):
                n = ci * npair + j
                h = ht[d][j]
                o_refs[d][ci * c:(ci + 1) * c, j * LANES:(j + 1) * LANES] = (
                    loc[d]["yloc"][n] + _mm_nt(loc[d]["qt"][n], h))
                ht[d][j] = ((_mm_nt(h, loc[d]["mt"][n]) + loc[d]["nt"][n])
                            * loc[d]["gam"][ci][:, j * LANES:(j + 1) * LANES])
            yield
    for d in range(2):
        for j in range(npair):
            st_ref[d, j] = ht[d][j]


def _rwkv_operands(ps, w0, w_up, a0, a_up, g_up, k_k, k_a, r_k, block_t, nb):
    w = ps.shape[-1]
    dm = RWKV_DIM
    c = RWKV_CHUNK
    t = np.arange(block_t)
    same = (t[:, None] // c) == (t[None, :] // c)
    tri = np.stack([same & (t[None, :] <= t[:, None]), same & (t[None, :] >= t[:, None])]).astype(np.float32)
    l = np.arange(LANES)
    seg = np.tile(((l[:, None] // RWKV_HEAD) == (l[None, :] // RWKV_HEAD)).astype(np.float32), (2, 1))
    rank = w_up.shape[1]
    wup_pad = jnp.zeros((2, LANES, dm), F32).at[:, :rank].set(w_up).astype(BF16)
    aup_pad = jnp.zeros((LANES, dm), F32).at[rank:rank + a_up.shape[0]].set(a_up).astype(BF16)
    row = lambda x: x.reshape(1, dm)
    args = [ps, ps, w0.reshape(2, 1, dm), wup_pad, row(a0), aup_pad, g_up.astype(BF16), row(k_k), row(k_a),
            row(r_k), jnp.asarray(seg, BF16), jnp.asarray(tri, BF16)]
    specs = [_fwd_spec(block_t, w), _bwd_spec(block_t, w, nb)] + [_const_spec(a.shape) for a in args[2:]]
    return args, specs, pltpu.VMEM((2, dm // LANES, LANES, LANES), F32)


SCAN_UNIT_COST = {"rwkv": 1.0 / 119, "hgrn": 1.0 / 16, "ssd": 1.0 / 20}


def _interleave(gens):
    spent = {name: 0.0 for name in gens}
    live = dict(gens)
    while live:
        name = min(live, key=lambda n: spent[n])
        try:
            next(live[name])
            spent[name] += SCAN_UNIT_COST[name]
        except StopIteration:
            del live[name]


def _scans_kernel(*refs, n_in, n_out):
    ins, outs, (s_st, r_st, h_st) = refs[:sum(n_in)], refs[sum(n_in):sum(n_in) + sum(n_out)], refs[-3:]

    @pl.when(pl.program_id(1) == 0)
    def _():
        s_st[...] = jnp.zeros_like(s_st)
        r_st[...] = jnp.zeros_like(r_st)
        h_st[...] = jnp.zeros_like(h_st)

    s_in, r_in, h_in = ins[:n_in[0]], ins[n_in[0]:n_in[0] + n_in[1]], ins[n_in[0] + n_in[1]:]
    s_out, r_out, h_out = outs[:n_out[0]], outs[n_out[0]:n_out[0] + n_out[1]], outs[n_out[0] + n_out[1]:]
    _interleave({"rwkv": _rwkv_body(*r_in, *r_out, r_st),
                 "hgrn": _hgrn_body(*h_in, *h_out, h_st),
                 "ssd": _ssd_body(*s_in, *s_out, s_st)})


def _scans(ssd_in, rwkv_in, hgrn_in, batch, seq_len, block_t):
    nb = seq_len // block_t
    groups = [_ssd_operands(*ssd_in, block_t, nb), _rwkv_operands(*rwkv_in, block_t, nb),
              _hgrn_operands(*hgrn_in, block_t, nb)]
    fwd, bwd = (lambda w: _fwd_spec(block_t, w)), (lambda w: _bwd_spec(block_t, w, nb))
    hd = hgrn_in[0].shape[-1] // 4
    out_specs = [fwd(SSD_INNER), bwd(SSD_INNER),
                 fwd(RWKV_DIM), bwd(RWKV_DIM), fwd(RWKV_DIM), fwd(RWKV_DIM),
                 fwd(hd), bwd(hd)]
    out_shape = [jax.ShapeDtypeStruct((batch, seq_len, s.block_shape[-1]), F32) for s in out_specs]
    return pl.pallas_call(
        functools.partial(_scans_kernel, n_in=tuple(len(g[0]) for g in groups), n_out=(2, 4, 2)),
        grid=(batch, nb),
        in_specs=[s for g in groups for s in g[1]],
        out_specs=out_specs,
        out_shape=out_shape,
        scratch_shapes=[g[2] for g in groups],
        compiler_params=pltpu.CompilerParams(
            dimension_semantics=("arbitrary", "arbitrary"), vmem_limit_bytes=VMEM_LIMIT_BYTES),
        name="scans",
    )(*[a for g in groups for a in g[0]])


def _rms(x, w):
    return x * lax.rsqrt(jnp.mean(x * x, axis=-1, keepdims=True) + NORM_EPS) * w


def _in_proj_kernel(x_ref, xp_ref, xn_ref, nw_ref, wloc_ref, wpt_ref, cw_ref, cb_ref, mu_ref,
                    xact_ref, ps_ref, dt_ref, ph_ref, *, blocks_per_seq):
    i = pl.program_id(0)
    t = x_ref.shape[0]
    te = t + 2 * HALO
    dims = ((1,), (0,))
    xe = _rms(jnp.concatenate([xp_ref[...], x_ref[...], xn_ref[...]], axis=0), nw_ref[...])
    pe = _dot(xe.astype(BF16), wloc_ref[...], dims)
    pt = _dot(xe[HALO:HALO + t].astype(BF16), wpt_ref[...], dims)
    dt_ref[...] = pt[:, :LANES]
    ph_ref[...] = pt[:, LANES:]

    first = (i % blocks_per_seq) == 0
    last = (i % blocks_per_seq) == blocks_per_seq - 1
    pe = jnp.concatenate([jnp.where(first, 0.0, pe[:HALO]), pe[HALO:HALO + t],
                          jnp.where(last, 0.0, pe[HALO + t:])], axis=0)
    wc = cw_ref.shape[1]
    xb, pr = pe[:, :wc], pe[:, wc:]
    half = (SSD_CONV_TAPS - 1) // 2
    acc = cb_ref[...] + cw_ref[half:half + 1, :] * xb[HALO:HALO + t]
    for j in range(SSD_CONV_TAPS):
        if j != half:
            acc = acc + cw_ref[j:j + 1, :] * pltpu.roll(xb, (half - j) % te, 0)[HALO:HALO + t]
    xact_ref[...] = _silu(acc)
    p = pr[HALO:HALO + t]
    prev = pltpu.roll(pr, 1, 0)[HALO:HALO + t]
    nxt = pltpu.roll(pr, te - 1, 0)[HALO:HALO + t]
    ps_ref[...] = p + mu_ref[0:1, :] * (prev - p) + mu_ref[1:2, :] * (nxt - p)


def _in_proj(x, norm_w, w_loc, w_pt, conv_w, conv_b, mu, seq_len, block_t):
    t, dmodel = x.shape
    r = block_t // HALO
    last = t // HALO - 1
    wc = conv_w.shape[-1]
    taps = jnp.zeros((SUBLANES, wc), F32).at[:SSD_CONV_TAPS].set(conv_w[:, 0, :])
    tok = lambda wd: pl.BlockSpec((block_t, wd), lambda i: (i, 0))
    widths = (wc, w_loc.shape[1] - wc, LANES, w_pt.shape[1] - LANES)
    return pl.pallas_call(
        functools.partial(_in_proj_kernel, blocks_per_seq=seq_len // block_t),
        grid=(t // block_t,),
        in_specs=[tok(dmodel),
                  pl.BlockSpec((HALO, dmodel), lambda i: (jnp.maximum(i * r - 1, 0), 0)),
                  pl.BlockSpec((HALO, dmodel), lambda i: (jnp.minimum((i + 1) * r, last), 0)),
                  _const_spec((1, dmodel)), _const_spec(w_loc.shape), _const_spec(w_pt.shape),
                  _const_spec(taps.shape), _const_spec((1, wc)), _const_spec(mu.shape)],
        out_specs=[tok(wd) for wd in widths],
        out_shape=[jax.ShapeDtypeStruct((t, wd), F32) for wd in widths],
        compiler_params=pltpu.CompilerParams(
            dimension_semantics=("arbitrary",), vmem_limit_bytes=VMEM_LIMIT_BYTES),
        name="in_proj",
    )(x, x, x, norm_w.reshape(1, dmodel), w_loc, w_pt, taps, conv_b.reshape(1, wc), mu)


def _merge_kernel(x_ref, sf_ref, sb_ref, rf_ref, rb_ref, rg_ref, rbon_ref, hf_ref, hb_ref,
                  nw1_ref, wz_ref, whg_ref, wgate_ref, snw_ref, lnw_ref, lnb_ref, hnw_ref, seg_ref,
                  wbs_ref, wbr_ref, wbh_ref, wout_ref, o_ref):
    dmodel = x_ref.shape[1]
    dims = ((1,), (0,))
    xn = _rms(x_ref[...], nw1_ref[...]).astype(BF16)
    ys = (sf_ref[...] + sb_ref[...]) * _silu(_dot(xn, wz_ref[...], dims))
    y1 = _mm(_rms(ys, snw_ref[...]), wbs_ref[...])

    seg = seg_ref[...]
    yr = rf_ref[...] + rb_ref[...]
    cen = yr - _seg_sum(yr, seg) * (1.0 / RWKV_HEAD)
    var = _seg_sum(cen * cen, seg) * (1.0 / RWKV_HEAD)
    yn = cen * lax.rsqrt(var + RWKV_LN_EPS) * lnw_ref[...] + lnb_ref[...]
    y2 = _mm((yn + rbon_ref[...]) * rg_ref[...], wbr_ref[...])

    yh = hf_ref[...] + hb_ref[...]
    yh = jnp.concatenate(
        [_rms(yh[:, j * HGRN_HEAD:(j + 1) * HGRN_HEAD], hnw_ref[:, j * HGRN_HEAD:(j + 1) * HGRN_HEAD])
         for j in range(yh.shape[1] // HGRN_HEAD)], axis=1)
    y3 = _mm(yh * _silu(_dot(xn, whg_ref[...], dims)), wbh_ref[...])

    gates = _sigmoid(_dot(xn, wgate_ref[...], dims))
    mixed = (gates[:, 0:dmodel] * y1 + gates[:, dmodel:2 * dmodel] * y2 + gates[:, 2 * dmodel:3 * dmodel] * y3)
    o_ref[...] = x_ref[...] + _mm(mixed, wout_ref[...])


def _merge(x, sf, sb, rf, rb, rg, rbon, hf, hb, norm1_w, w_z, w_hg, w_gate, ssm_norm_w, ln_w, ln_b,
           hgrn_norm_w, wb_ssm, wb_rwkv, wb_hgrn, w_out, block_t):
    t, dmodel = x.shape
    hd = hf.shape[1]
    l = np.arange(LANES)
    seg = jnp.asarray(np.tile((l[:, None] // RWKV_HEAD) == (l[None, :] // RWKV_HEAD), (2, 1)), BF16)
    tok = lambda w: pl.BlockSpec((block_t, w), lambda i: (i, 0))
    full = lambda a: _const_spec(a.shape)
    params = [norm1_w.reshape(1, -1), w_z, w_hg, w_gate, ssm_norm_w.reshape(1, -1), ln_w.reshape(1, -1),
              ln_b.reshape(1, -1), hgrn_norm_w.reshape(1, -1), seg, wb_ssm, wb_rwkv, wb_hgrn, w_out]
    return pl.pallas_call(
        _merge_kernel,
        grid=(t // block_t,),
        in_specs=[tok(dmodel), tok(dmodel), tok(dmodel), tok(hd), tok(hd), tok(hd), tok(hd), tok(hd), tok(hd)]
        + [full(a) for a in params],
        out_specs=tok(dmodel),
        out_shape=jax.ShapeDtypeStruct((t, dmodel), F32),
        compiler_params=pltpu.CompilerParams(
            dimension_semantics=("arbitrary",), vmem_limit_bytes=VMEM_LIMIT_BYTES),
        name="merge",
    )(x, sf, sb, rf, rb, rg, rbon, hf, hb, *params)


FFN_SPLIT = 2


def _ffn_kernel(x_ref, nw_ref, win_ref, wdown_ref, fw_ref, o_ref, *, final):
    x = x_ref[...]
    xn = _rms(x, nw_ref[...]).astype(BF16)
    dff = wdown_ref.shape[0]
    step = dff // FFN_SPLIT
    acc = x
    for c in range(FFN_SPLIT):
        hg = _dot(xn, win_ref[:, c * step:(c + 1) * step], ((1,), (0,)))
        hu = _dot(xn, win_ref[:, dff + c * step:dff + (c + 1) * step], ((1,), (0,)))
        acc = acc + _mm(_silu(hg) * hu, wdown_ref[c * step:(c + 1) * step, :])
    o_ref[...] = _rms(acc, fw_ref[...]) if final else acc


def _ffn(x, norm_w, w_in, w_down, final_w, final, block_t):
    t, dmodel = x.shape
    tok = pl.BlockSpec((block_t, dmodel), lambda i: (i, 0))
    return pl.pallas_call(
        functools.partial(_ffn_kernel, final=final),
        grid=(t // block_t,),
        in_specs=[tok, _const_spec((1, dmodel)), _const_spec(w_in.shape), _const_spec(w_down.shape),
                  _const_spec((1, dmodel))],
        out_specs=tok,
        out_shape=jax.ShapeDtypeStruct((t, dmodel), F32),
        compiler_params=pltpu.CompilerParams(
            dimension_semantics=("arbitrary",), vmem_limit_bytes=VMEM_LIMIT_BYTES),
        name="ffn",
    )(x, norm_w.reshape(1, dmodel), w_in, w_down, final_w.reshape(1, dmodel))


SCAN_BLOCK_T = 256
DENSE_BLOCK_T = 256
FFN_BLOCK_T = 512


def kernel(x, norm1_w, w_in, ssm_conv_w, ssm_conv_b, ssm_dt_bias, ssm_a_log, ssm_d, ssm_norm_w, rwkv_mu, rwkv_w0, rwkv_w_up, rwkv_a0, rwkv_a_up, rwkv_g_up, rwkv_k_k, rwkv_k_a, rwkv_r_k, rwkv_ln_w, rwkv_ln_b, hgrn_lb_logits, hgrn_norm_w, w_branch_ssm, w_branch_rwkv, w_branch_hgrn, w_out, norm2_w, ffn_w_in, ffn_w_down, final_norm_w):
    b, L, dmodel = x.shape
    depth = w_in.shape[0]
    lb_p = jax.nn.softmax(hgrn_lb_logits.astype(F32), axis=0)
    lower_bounds = jnp.cumsum(lb_p, axis=0) - lb_p[0]
    conv_dim = ssm_conv_w.shape[-1]
    n_dt = ssm_dt_bias.shape[1] * ssm_dt_bias.shape[2]
    rw, hw = rwkv_mu.shape[-1], 5 * hgrn_norm_w.shape[-1]
    bounds = np.cumsum([0, SSD_INNER, conv_dim, n_dt, rw, hw, 3 * dmodel])
    seq = lambda a: a.reshape(b, L, a.shape[-1])
    xt = x.reshape(b * L, dmodel)
    for l in range(depth):
        wz, wxbc, wdt, wrw, whg, wgate = (w_in[l][:, bounds[i]:bounds[i + 1]] for i in range(6))
        hd = hw // 5
        w_loc = jnp.concatenate([wxbc, wrw], axis=1).astype(BF16)
        w_pt = jnp.concatenate([wdt] * SSD_DT_COPIES + [jnp.zeros((dmodel, LANES - SSD_DT_COPIES * n_dt), F32),
                                                       whg[:, :4 * hd]], axis=1).astype(BF16)
        wz, w_hg, wgate = wz.astype(BF16), whg[:, 4 * hd:].astype(BF16), wgate.astype(BF16)
        xact, ps, dtp, p_hgrn = _in_proj(xt, norm1_w[l], w_loc, w_pt, ssm_conv_w[l], ssm_conv_b[l], rwkv_mu[l],
                                         L, DENSE_BLOCK_T)
        sf, sb, rf, rb, rg, rbon, hf, hb = _scans(
            (seq(xact), seq(dtp), ssm_dt_bias[l], ssm_a_log[l], ssm_d[l]),
            (seq(ps), rwkv_w0[l], rwkv_w_up[l], rwkv_a0[l], rwkv_a_up[l], rwkv_g_up[l], rwkv_k_k[l], rwkv_k_a[l],
             rwkv_r_k[l]),
            (seq(p_hgrn), lower_bounds[l]), b, L, SCAN_BLOCK_T)
        flat = lambda a: a.reshape(b * L, a.shape[-1])
        xt = _merge(xt, flat(sf), flat(sb), flat(rf), flat(rb), flat(rg), flat(rbon), flat(hf), flat(hb),
                    norm1_w[l], wz, w_hg, wgate, ssm_norm_w[l], rwkv_ln_w[l], rwkv_ln_b[l], hgrn_norm_w[l],
                    w_branch_ssm[l].astype(BF16), w_branch_rwkv[l].astype(BF16), w_branch_hgrn[l].astype(BF16),
                    w_out[l].astype(BF16), DENSE_BLOCK_T)
        xt = _ffn(xt, norm2_w[l], ffn_w_in[l].astype(BF16), ffn_w_down[l].astype(BF16), final_norm_w,
                  l == depth - 1, FFN_BLOCK_T)
    return xt.reshape(b, L, dmodel)
```
